```python
import math
import jax, jax.numpy as jnp
from jax import lax
import numpy as np

D_MODEL = 1024
BATCH = 8
SEQ = 4096
DEPTH = 1

FOX_HEADS = 16
FOX_HEAD_DIM = 64
FOX_WIDTH = FOX_HEADS * FOX_HEAD_DIM
Q_BLOCK = 128
FORGET_BIAS_CENTER = 3.0

GLA_HEADS = 4
GLA_KEY_DIM = D_MODEL // 2
GLA_VAL_DIM = D_MODEL
GLA_HK = GLA_KEY_DIM // GLA_HEADS
GLA_HV = GLA_VAL_DIM // GLA_HEADS
GLA_GATE_RANK = 16
GLA_GATE_NORMALIZER = 16.0
GLA_CHUNK = 64

SPLIT_SIZES = (FOX_WIDTH, FOX_WIDTH, FOX_WIDTH, FOX_HEADS,
               GLA_KEY_DIM, GLA_KEY_DIM, GLA_VAL_DIM, GLA_GATE_RANK, GLA_VAL_DIM,
               D_MODEL, D_MODEL)
IN_COLS = int(sum(SPLIT_SIZES))
SPLIT_POINTS = tuple(int(p) for p in np.cumsum(SPLIT_SIZES)[:-1])

PEER_HEADS = 8
PEER_NKEYS = 128
PEER_N_EXPERTS = PEER_NKEYS * PEER_NKEYS
PEER_QDIM = 256
PEER_HALF = PEER_QDIM // 2
PEER_TOPK = 16
PEER_TOKEN_CHUNK = 128

ALPHA = (2.0 * DEPTH) ** 0.25
BETA = (8.0 * DEPTH) ** -0.25
LN_EPS = 1e-5
RMS_EPS = 1e-6

kernel_name = 'fox_gla_peer_deepnorm_hybrid'


def layer_norm(x, g, b):
    xf = x.astype(jnp.float32)
    mu = jnp.mean(xf, axis=-1, keepdims=True)
    var = jnp.mean(jnp.square(xf - mu), axis=-1, keepdims=True)
    y = (xf - mu) * lax.rsqrt(var + LN_EPS) * g.astype(jnp.float32) + b.astype(jnp.float32)
    return y.astype(x.dtype)


def rms_norm(x, g):
    xf = x.astype(jnp.float32)
    y = xf * lax.rsqrt(jnp.mean(jnp.square(xf), axis=-1, keepdims=True) + RMS_EPS) * g.astype(jnp.float32)
    return y.astype(x.dtype)


def fox_attention(q, k, v, f_logit, b_f):
    B, S, H, dh = q.shape
    logf = jax.nn.log_sigmoid((f_logit + b_f).astype(jnp.float32))
    c = jnp.cumsum(logf, axis=1).transpose(0, 2, 1)
    qh = q.transpose(0, 2, 1, 3)
    kh = k.transpose(0, 2, 1, 3)
    vh = v.transpose(0, 2, 1, 3)
    nb = S // Q_BLOCK
    q_blocks = qh.reshape(B, H, nb, Q_BLOCK, dh).transpose(2, 0, 1, 3, 4)
    c_blocks = c.reshape(B, H, nb, Q_BLOCK).transpose(2, 0, 1, 3)
    k_pos = jnp.arange(S)
    scale = FOX_HEAD_DIM ** -0.5

    def one_block(args):
        qb, cb, bi = args
        q_pos = bi * Q_BLOCK + jnp.arange(Q_BLOCK)
        s = jnp.einsum('bhqd,bhkd->bhqk', qb, kh).astype(jnp.float32) * scale
        s = s + (cb[..., :, None] - c[..., None, :])
        s = jnp.where(q_pos[:, None] >= k_pos[None, :], s, -1e30)
        p = jax.nn.softmax(s, axis=-1)
        return jnp.einsum('bhqk,bhkd->bhqd', p.astype(vh.dtype), vh)

    o = lax.map(one_block, (q_blocks, c_blocks, jnp.arange(nb)))
    return o.transpose(1, 0, 3, 2, 4).reshape(B, S, H * dh)


def gla_attention(q, k, v, log_a):
    B, S, H, dk = q.shape
    dv = v.shape[-1]
    nc = S // GLA_CHUNK

    def to_chunks(t):
        return t.astype(jnp.float32).reshape(B, nc, GLA_CHUNK, H, t.shape[-1]).transpose(1, 0, 3, 2, 4)

    qc = to_chunks(q * (GLA_HK ** -0.5))
    kc = to_chunks(k)
    vc = to_chunks(v)
    gc = to_chunks(log_a)
    causal = jnp.tril(jnp.ones((GLA_CHUNK, GLA_CHUNK), dtype=bool))

    def step(state, inp):
        qb, kb, vb, gb = inp
        bcum = jnp.cumsum(gb, axis=2)
        inter = jnp.einsum('bhcd,bhde->bhce', qb * jnp.exp(bcum), state)
        rel = bcum[:, :, :, None, :] - bcum[:, :, None, :, :]
        rel = jnp.where(causal[:, :, None], rel, -jnp.inf)
        attn = jnp.einsum('bhid,bhjd,bhijd->bhij', qb, kb, jnp.exp(rel))
        intra = jnp.einsum('bhij,bhje->bhie', attn, vb)
        b_last = bcum[:, :, -1]
        k_dec = kb * jnp.exp(b_last[:, :, None, :] - bcum)
        new_state = state * jnp.exp(b_last)[..., None] + jnp.einsum('bhcd,bhce->bhde', k_dec, vb)
        return new_state, inter + intra

    state0 = jnp.zeros((B, H, dk, dv), jnp.float32)
    _, o = lax.scan(step, state0, (qc, kc, vc, gc))
    return o.transpose(1, 0, 3, 2, 4).reshape(B, S, H, dv).astype(v.dtype)


def mixer_sublayer(x, w_in, b_fgate, w_gla_g2, b_gla_g2, gla_norm_g, w_br_fox, w_br_gla, w_o):
    B, S, _ = x.shape
    proj = x @ w_in
    fq, fk, fv, ff, gq, gk, gv, gg1, gog, zf, zg = jnp.split(proj, SPLIT_POINTS, axis=-1)
    fox = fox_attention(fq.reshape(B, S, FOX_HEADS, FOX_HEAD_DIM),
                        fk.reshape(B, S, FOX_HEADS, FOX_HEAD_DIM),
                        fv.reshape(B, S, FOX_HEADS, FOX_HEAD_DIM), ff, b_fgate)
    log_a = jax.nn.log_sigmoid((gg1 @ w_gla_g2 + b_gla_g2).astype(jnp.float32)) / GLA_GATE_NORMALIZER
    gla = gla_attention(gq.reshape(B, S, GLA_HEADS, GLA_HK),
                        gk.reshape(B, S, GLA_HEADS, GLA_HK),
                        gv.reshape(B, S, GLA_HEADS, GLA_HV),
                        log_a.reshape(B, S, GLA_HEADS, GLA_HK))
    gla = rms_norm(gla, gla_norm_g).reshape(B, S, GLA_VAL_DIM) * jax.nn.silu(gog)
    merged = jax.nn.sigmoid(zf) * (fox @ w_br_fox) + jax.nn.sigmoid(zg) * (gla @ w_br_gla)
    return merged @ w_o


def peer_sublayer(x, wq, subkeys, u_tab, v_tab):
    B, S, D = x.shape
    T = B * S
    xt = x.reshape(T, D)
    q = (xt @ wq).reshape(T, PEER_HEADS, 2, PEER_HALF)
    s = jnp.einsum('thpc,hpnc->thpn', q, subkeys).astype(jnp.float32)
    top_s, top_i = lax.top_k(s, PEER_TOPK)
    cand_s = top_s[:, :, 0, :, None] + top_s[:, :, 1, None, :]
    cand_i = top_i[:, :, 0, :, None] * PEER_NKEYS + top_i[:, :, 1, None, :]
    cand_s = cand_s.reshape(T, PEER_HEADS, PEER_TOPK * PEER_TOPK)
    cand_i = cand_i.reshape(T, PEER_HEADS, PEER_TOPK * PEER_TOPK)
    best_s, pos = lax.top_k(cand_s, PEER_TOPK)
    idx = jnp.take_along_axis(cand_i, pos, axis=-1)
    gate = jax.nn.softmax(best_s, axis=-1).astype(x.dtype)
    nchunk = T // PEER_TOKEN_CHUNK

    def one_chunk(args):
        xc, ic, gcf = args
        u = u_tab[ic]
        act = jax.nn.gelu(jnp.einsum('cd,chkd->chk', xc, u), approximate=False)
        v = v_tab[ic]
        return jnp.einsum('chk,chkd->cd', act * gcf, v)

    out = lax.map(one_chunk, (xt.reshape(nchunk, PEER_TOKEN_CHUNK, D),
                              idx.reshape(nchunk, PEER_TOKEN_CHUNK, PEER_HEADS, PEER_TOPK),
                              gate.reshape(nchunk, PEER_TOKEN_CHUNK, PEER_HEADS, PEER_TOPK)))
    return out.reshape(B, S, D)


def setup_inputs(seed: int = 0) -> dict:
    key = jax.random.key(seed)
    ks = jax.random.split(key, 20)
    D = D_MODEL

    def nrm(k, shape, scale):
        return jax.random.normal(k, shape, jnp.float32) * scale

    x = nrm(ks[0], (BATCH, SEQ, D), 1.0)
    col_scale = (1.0, 1.0, BETA, 1.0, 1.0, 1.0, BETA, 1.0, 1.0, 1.0, 1.0)
    col_vec = jnp.concatenate([jnp.full((n,), s, jnp.float32) for n, s in zip(SPLIT_SIZES, col_scale)])
    w_in = nrm(ks[1], (DEPTH, D, IN_COLS), D ** -0.5) * col_vec
    b_fgate = FORGET_BIAS_CENTER + nrm(ks[2], (DEPTH, FOX_HEADS), 0.5)
    w_gla_g2 = nrm(ks[3], (DEPTH, GLA_GATE_RANK, GLA_KEY_DIM), GLA_GATE_RANK ** -0.5)
    b_gla_g2 = nrm(ks[4], (DEPTH, GLA_KEY_DIM), 0.1)
    gla_norm_g = 1.0 + nrm(ks[5], (DEPTH, GLA_HV), 0.05)
    w_br_fox = nrm(ks[6], (DEPTH, FOX_WIDTH, D), FOX_WIDTH ** -0.5 * BETA)
    w_br_gla = nrm(ks[7], (DEPTH, GLA_VAL_DIM, D), GLA_VAL_DIM ** -0.5 * BETA)
    w_o = nrm(ks[8], (DEPTH, D, D), D ** -0.5 * BETA)
    ln1_g = 1.0 + nrm(ks[9], (DEPTH, D), 0.05)
    ln1_b = nrm(ks[10], (DEPTH, D), 0.02)
    peer_wq = nrm(ks[11], (DEPTH, D, PEER_HEADS * PEER_QDIM), D ** -0.5)
    peer_subkeys = nrm(ks[12], (DEPTH, PEER_HEADS, 2, PEER_NKEYS, PEER_HALF), PEER_HALF ** -0.5)
    peer_u = nrm(ks[13], (DEPTH, PEER_N_EXPERTS, D), D ** -0.5)
    peer_v = nrm(ks[14], (DEPTH, PEER_N_EXPERTS, D), BETA)
    ln2_g = 1.0 + nrm(ks[15], (DEPTH, D), 0.05)
    ln2_b = nrm(ks[16], (DEPTH, D), 0.02)
    return {'x': x, 'w_in': w_in, 'b_fgate': b_fgate, 'w_gla_g2': w_gla_g2, 'b_gla_g2': b_gla_g2,
            'gla_norm_g': gla_norm_g, 'w_br_fox': w_br_fox, 'w_br_gla': w_br_gla, 'w_o': w_o,
            'ln1_g': ln1_g, 'ln1_b': ln1_b, 'peer_wq': peer_wq, 'peer_subkeys': peer_subkeys,
            'peer_u': peer_u, 'peer_v': peer_v, 'ln2_g': ln2_g, 'ln2_b': ln2_b}


def reference(x, w_in, b_fgate, w_gla_g2, b_gla_g2, gla_norm_g, w_br_fox, w_br_gla, w_o,
              ln1_g, ln1_b, peer_wq, peer_subkeys, peer_u, peer_v, ln2_g, ln2_b):
    for l in range(DEPTH):
        mix = mixer_sublayer(x, w_in[l], b_fgate[l], w_gla_g2[l], b_gla_g2[l], gla_norm_g[l],
                             w_br_fox[l], w_br_gla[l], w_o[l])
        x = layer_norm(ALPHA * x + mix, ln1_g[l], ln1_b[l])
        ch = peer_sublayer(x, peer_wq[l], peer_subkeys[l], peer_u[l], peer_v[l])
        x = layer_norm(ALPHA * x + ch, ln2_g[l], ln2_b[l])
    return x
```

```python
import functools

import numpy as np
import jax
import jax.numpy as jnp
from jax import lax
from jax.experimental import pallas as pl
from jax.experimental.pallas import tpu as pltpu

F32 = jnp.float32
BF16 = jnp.bfloat16

D_MODEL = 1024
FOX_HEADS = 16
FOX_HEAD_DIM = 64
GLA_HEADS = 4
GLA_HK = 128
GLA_HV = 256
GLA_GATE_RANK = 16
GLA_GATE_NORMALIZER = 16.0
PEER_HEADS = 8
PEER_NKEYS = 128
PEER_HALF = 128
PEER_TOPK = 16
LN_EPS = 1e-5
RMS_EPS = 1e-6

LANES = 128
VMEM_LIMIT = 56 * 1024 * 1024

AUG_HI, AUG_MID, AUG_LO, AUG_ONE = 0, 16, 32, 48
NEG_BIG = -1e30


def _cparams(*sem):
    return pltpu.CompilerParams(dimension_semantics=sem, vmem_limit_bytes=VMEM_LIMIT)


def _log_sigmoid(z):
    return jnp.minimum(z, 0.0) - jnp.log1p(jnp.exp(-jnp.abs(z)))


def _layer_norm(r, g, b):
    mu = jnp.mean(r, axis=-1, keepdims=True)
    d = r - mu
    var = jnp.mean(d * d, axis=-1, keepdims=True)
    return d * lax.rsqrt(var + LN_EPS) * g + b


def _dot(a, b):
    return jnp.dot(a, b, preferred_element_type=F32)


def _dot_nt(a, b):
    return lax.dot_general(a, b, (((1,), (1,)), ((), ())), preferred_element_type=F32)


def _dot_tn(a, b):
    return lax.dot_general(a, b, (((0,), (0,)), ((), ())), preferred_element_type=F32)


def _proj_rest_kernel(x_ref, w_ref, ws_ref, o_ref, os_ref):
    xb = x_ref[...].astype(BF16)
    o_ref[...] = _dot(xb, w_ref[...]).astype(o_ref.dtype)

    @pl.when(pl.program_id(1) == 0)
    def _():
        os_ref[...] = _dot(xb, ws_ref[...])


def _proj_fox_kernel(x_ref, xa_ref, w_ref, wa_ref, o_ref):
    acc = _dot(x_ref[...].astype(BF16), w_ref[...]) + _dot(xa_ref[...], wa_ref[...])
    o_ref[...] = acc.astype(o_ref.dtype)


def _proj_rest(x2, w_rest, w_small, tm, tn):
    t, d = x2.shape
    n = w_rest.shape[1]
    return pl.pallas_call(
        _proj_rest_kernel,
        grid=(t // tm, n // tn),
        in_specs=[
            pl.BlockSpec((tm, d), lambda i, j: (i, 0)),
            pl.BlockSpec((d, tn), lambda i, j: (0, j)),
            pl.BlockSpec((d, LANES), lambda i, j: (0, 0)),
        ],
        out_specs=[
            pl.BlockSpec((tm, tn), lambda i, j: (i, j)),
            pl.BlockSpec((tm, LANES), lambda i, j: (i, 0)),
        ],
        out_shape=[
            jax.ShapeDtypeStruct((t, n), BF16),
            jax.ShapeDtypeStruct((t, LANES), F32),
        ],
        compiler_params=_cparams("parallel", "arbitrary"),
        name="proj_rest",
    )(x2, w_rest, w_small)


def _proj_fox(x2, xaug, w_fox, w_aug, tm, tn):
    t, d = x2.shape
    n = w_fox.shape[1]
    return pl.pallas_call(
        _proj_fox_kernel,
        grid=(t // tm, n // tn),
        in_specs=[
            pl.BlockSpec((tm, d), lambda i, j: (i, 0)),
            pl.BlockSpec((tm, LANES), lambda i, j: (i, 0)),
            pl.BlockSpec((d, tn), lambda i, j: (0, j)),
            pl.BlockSpec((LANES, tn), lambda i, j: (0, j)),
        ],
        out_specs=pl.BlockSpec((tm, tn), lambda i, j: (i, j)),
        out_shape=jax.ShapeDtypeStruct((t, n), BF16),
        compiler_params=_cparams("parallel", "arbitrary"),
        name="proj_fox",
    )(x2, xaug, w_fox, w_aug)


def _cumsum_kernel(s_ref, bias_ref, o_ref, carry_ref, *, tc):
    @pl.when(pl.program_id(1) == 0)
    def _():
        carry_ref[...] = jnp.zeros_like(carry_ref)

    z = s_ref[...] + bias_ref[...]
    lane = lax.broadcasted_iota(jnp.int32, z.shape, 1)
    logf = jnp.where(lane < FOX_HEADS, _log_sigmoid(z), 0.0)
    row = lax.broadcasted_iota(jnp.int32, (tc, tc), 0)
    col = lax.broadcasted_iota(jnp.int32, (tc, tc), 1)
    ltri = (row >= col).astype(F32)
    c = jnp.dot(ltri, logf, precision=lax.Precision.HIGHEST,
                preferred_element_type=F32) + carry_ref[0:1, :]
    carry_ref[...] = jnp.broadcast_to(c[tc - 1:tc, :], carry_ref.shape)
    c_mid = pltpu.roll(c, AUG_MID, 1)
    c_lo = pltpu.roll(c, AUG_LO, 1)
    csel = jnp.where(lane < AUG_MID, c, jnp.where(lane < AUG_LO, c_mid, c_lo))
    hi = csel.astype(BF16).astype(F32)
    r1 = csel - hi
    mid = r1.astype(BF16).astype(F32)
    lo = (r1 - mid).astype(BF16).astype(F32)
    val = jnp.where(lane < AUG_MID, hi,
                    jnp.where(lane < AUG_LO, mid,
                              jnp.where(lane < AUG_ONE, lo,
                                        jnp.where(lane == AUG_ONE, 1.0, 0.0))))
    o_ref[...] = val.astype(o_ref.dtype)


def _forget_cumsum(small, bias, batch, seq, tc):
    t = small.shape[0]
    nb = seq // tc
    return pl.pallas_call(
        functools.partial(_cumsum_kernel, tc=tc),
        grid=(batch, nb),
        in_specs=[
            pl.BlockSpec((tc, LANES), lambda b, s: (b * nb + s, 0)),
            pl.BlockSpec((1, LANES), lambda b, s: (0, 0)),
        ],
        out_specs=pl.BlockSpec((tc, LANES), lambda b, s: (b * nb + s, 0)),
        out_shape=jax.ShapeDtypeStruct((t, LANES), BF16),
        scratch_shapes=[pltpu.VMEM((8, LANES), F32)],
        compiler_params=_cparams("parallel", "arbitrary"),
        name="forget_cumsum",
    )(small, bias)


def _fox_kernel(q_ref, k_ref, v_ref, o_ref, m_ref, acc_ref, *, tq):
    qi = pl.program_id(2)
    lane = lax.broadcasted_iota(jnp.int32, (tq, LANES), 1)
    row = lax.broadcasted_iota(jnp.int32, (tq, tq), 0)
    col = lax.broadcasted_iota(jnp.int32, (tq, tq), 1)
    causal = row >= col
    outs = []
    for hh in range(2):
        lanes = slice(hh * LANES, (hh + 1) * LANES)
        q = q_ref[:, lanes]
        m_ref[...] = jnp.full(m_ref.shape, NEG_BIG, F32)
        acc_ref[...] = jnp.zeros(acc_ref.shape, F32)

        def step(kb, masked):
            start = pl.multiple_of(kb * tq, tq)
            k = k_ref[pl.ds(start, tq), lanes]
            v = v_ref[pl.ds(start, tq), lanes]
            s = _dot_nt(q, k)
            if masked:
                s = jnp.where(causal, s, NEG_BIG)
            m_prev = m_ref[...]
            m_next = jnp.maximum(m_prev, jnp.max(s, axis=1, keepdims=True))
            p = jnp.exp(s - jnp.tile(m_next, (1, tq // LANES)))
            alpha = jnp.exp(m_prev - m_next)
            acc_ref[...] = acc_ref[...] * alpha + _dot(p.astype(BF16), v)
            m_ref[...] = m_next

        def body(kb, carry):
            step(kb, False)
            return carry

        lax.fori_loop(0, qi, body, 0)
        step(qi, True)
        acc = acc_ref[...]
        l_lane = 64 if hh == 0 else 0
        l = jnp.sum(jnp.where(lane == l_lane, acc, 0.0), axis=1, keepdims=True)
        outs.append(acc / l)
    o_ref[...] = jnp.where(lane < 64, outs[0], outs[1]).astype(o_ref.dtype)


def _fox_attention(qkv, batch, seq, tq):
    t = qkv.shape[0]
    nq = seq // tq
    npair = FOX_HEADS // 2
    return pl.pallas_call(
        functools.partial(_fox_kernel, tq=tq),
        grid=(batch, npair, nq),
        in_specs=[
            pl.BlockSpec((tq, 2 * LANES), lambda b, h, i: (b * nq + i, h)),
            pl.BlockSpec((seq, 2 * LANES), lambda b, h, i: (b, npair + h)),
            pl.BlockSpec((seq, 2 * LANES), lambda b, h, i: (b, 2 * npair + h)),
        ],
        out_specs=pl.BlockSpec((tq, LANES), lambda b, h, i: (b * nq + i, h)),
        out_shape=jax.ShapeDtypeStruct((t, FOX_HEADS * FOX_HEAD_DIM), BF16),
        scratch_shapes=[pltpu.VMEM((tq, LANES), F32), pltpu.VMEM((tq, LANES), F32)],
        compiler_params=_cparams("parallel", "parallel", "arbitrary"),
        name="fox_attention",
    )(qkv, qkv, qkv)


def _gla_kernel(q_ref, k_ref, v_ref, sm_ref, gog_ref, wg_ref, bg_ref, ng_ref, o_ref,
                st_ref, *, rows, chunk):
    @pl.when(pl.program_id(2) == 0)
    def _():
        st_ref[...] = jnp.zeros_like(st_ref)

    z = _dot(sm_ref[...].astype(BF16), wg_ref[...]) + bg_ref[...]
    log_a = _log_sigmoid(z) * (1.0 / GLA_GATE_NORMALIZER)
    ri = lax.broadcasted_iota(jnp.int32, (chunk, chunk), 0)
    ci = lax.broadcasted_iota(jnp.int32, (chunk, chunk), 1)
    tril = ri >= ci
    ltri = tril.astype(F32)
    for c in range(rows // chunk):
        sl = slice(c * chunk, (c + 1) * chunk)
        bc = jnp.dot(ltri, log_a[sl], precision=lax.Precision.HIGHEST,
                     preferred_element_type=F32)
        b_last = bc[chunk - 1:chunk, :]
        kf = k_ref[sl, :].astype(F32)
        qe = (q_ref[sl, :].astype(F32) * (GLA_HK ** -0.5) * jnp.exp(bc)).astype(BF16)
        kd = (kf * jnp.exp(-bc)).astype(BF16)
        kdec = (kf * jnp.exp(b_last - bc)).astype(BF16)
        v = v_ref[sl, :]
        attn = jnp.where(tril, _dot_nt(qe, kd), 0.0)
        st = st_ref[...]
        o = _dot(attn.astype(BF16), v) + _dot_nt(qe, st.astype(BF16))
        st_ref[...] = st * jnp.exp(b_last) + _dot_tn(v, kdec)
        ms = jnp.mean(o * o, axis=-1, keepdims=True)
        gate = gog_ref[sl, :].astype(F32)
        y = o * lax.rsqrt(ms + RMS_EPS) * ng_ref[...] * (gate * jax.nn.sigmoid(gate))
        o_ref[sl, :] = y.astype(o_ref.dtype)


def _gla(rest, small, w_g2, b_g2, norm_g, batch, seq, rows, chunk):
    t = rest.shape[0]
    nr = seq // rows
    kq = 0
    kk = GLA_HEADS * GLA_HK // GLA_HK
    kv = 2 * GLA_HEADS * GLA_HK // GLA_HV
    kg = kv + GLA_HEADS
    return pl.pallas_call(
        functools.partial(_gla_kernel, rows=rows, chunk=chunk),
        grid=(batch, GLA_HEADS, nr),
        in_specs=[
            pl.BlockSpec((rows, GLA_HK), lambda b, h, s: (b * nr + s, kq + h)),
            pl.BlockSpec((rows, GLA_HK), lambda b, h, s: (b * nr + s, kk + h)),
            pl.BlockSpec((rows, GLA_HV), lambda b, h, s: (b * nr + s, kv + h)),
            pl.BlockSpec((rows, LANES), lambda b, h, s: (b * nr + s, 0)),
            pl.BlockSpec((rows, GLA_HV), lambda b, h, s: (b * nr + s, kg + h)),
            pl.BlockSpec((LANES, GLA_HK), lambda b, h, s: (0, h)),
            pl.BlockSpec((1, GLA_HK), lambda b, h, s: (0, h)),
            pl.BlockSpec((1, GLA_HV), lambda b, h, s: (0, 0)),
        ],
        out_specs=pl.BlockSpec((rows, GLA_HV), lambda b, h, s: (b * nr + s, h)),
        out_shape=jax.ShapeDtypeStruct((t, GLA_HEADS * GLA_HV), BF16),
        scratch_shapes=[pltpu.VMEM((GLA_HV, GLA_HK), F32)],
        compiler_params=_cparams("parallel", "parallel", "arbitrary"),
        name="gla",
    )(rest, rest, rest, small, rest, w_g2, b_g2, norm_g)


def _merge_kernel(x_ref, fox_ref, gla_ref, zf_ref, zg_ref, wf_ref, wg_ref, wo_ref,
                  g_ref, b_ref, o_ref, *, alpha):
    a = _dot(fox_ref[...], wf_ref[...])
    b = _dot(gla_ref[...], wg_ref[...])
    merged = (jax.nn.sigmoid(zf_ref[...].astype(F32)) * a
              + jax.nn.sigmoid(zg_ref[...].astype(F32)) * b)
    y = _dot(merged.astype(BF16), wo_ref[...])
    o_ref[...] = _layer_norm(alpha * x_ref[...] + y, g_ref[...], b_ref[...])


def _merge(x2, fox, gla, rest, wf, wg, wo, ln_g, ln_b, alpha, tm):
    t, d = x2.shape
    zf_blk = (rest.shape[1] - 2 * d) // d
    full = lambda i: (0, 0)
    return pl.pallas_call(
        functools.partial(_merge_kernel, alpha=alpha),
        grid=(t // tm,),
        in_specs=[
            pl.BlockSpec((tm, d), lambda i: (i, 0)),
            pl.BlockSpec((tm, d), lambda i: (i, 0)),
            pl.BlockSpec((tm, d), lambda i: (i, 0)),
            pl.BlockSpec((tm, d), lambda i: (i, zf_blk)),
            pl.BlockSpec((tm, d), lambda i: (i, zf_blk + 1)),
            pl.BlockSpec((d, d), full),
            pl.BlockSpec((d, d), full),
            pl.BlockSpec((d, d), full),
            pl.BlockSpec((1, d), full),
            pl.BlockSpec((1, d), full),
        ],
        out_specs=pl.BlockSpec((tm, d), lambda i: (i, 0)),
        out_shape=jax.ShapeDtypeStruct((t, d), F32),
        compiler_params=_cparams("parallel"),
        name="merge_ln",
    )(x2, fox, gla, rest, rest, wf, wg, wo, ln_g, ln_b)


def _top_rows(s, k):
    tm = s.shape[1]
    row = lax.broadcasted_iota(jnp.int32, (k, tm), 0)
    out = jnp.full((k, tm), -jnp.inf, F32)
    cur = s
    for r in range(k):
        m = jnp.max(cur, axis=0, keepdims=True)
        out = jnp.where(row == r, m, out)
        cur = jnp.where(cur == m, -jnp.inf, cur)
    return out


def _peerq_kernel(x_ref, wq_ref, sk_ref, s2_ref, a_ref, th_ref, *, tm):
    xb = x_ref[...].astype(BF16)
    qt = _dot_nt(wq_ref[...], xb)
    row8 = lax.broadcasted_iota(jnp.int32, (8, tm), 0)
    for h in range(PEER_HEADS):
        q1 = qt[(2 * h) * PEER_HALF:(2 * h + 1) * PEER_HALF].astype(BF16)
        q2 = qt[(2 * h + 1) * PEER_HALF:(2 * h + 2) * PEER_HALF].astype(BF16)
        s1 = _dot(sk_ref[2 * h], q1)
        s2 = _dot(sk_ref[2 * h + 1], q2)
        t1 = _top_rows(s1, PEER_TOPK)
        t2 = _top_rows(s2, PEER_TOPK)
        blocks = [t1[0:1] + t2]
        for a in range(1, 8):
            nb = PEER_TOPK // (a + 1)
            blocks.append(jnp.where(row8 < nb, t1[a:a + 1] + t2[0:8], -jnp.inf))
        blocks.append(t1[8:16] + t2[0:1])
        cand = jnp.concatenate(blocks, axis=0)
        best = _top_rows(cand, PEER_TOPK)
        tau = best[PEER_TOPK - 1:PEER_TOPK]
        zsum = jnp.sum(jnp.exp(best - best[0:1]), axis=0, keepdims=True)
        theta = jnp.full(s1.shape, jnp.inf, F32)
        for a in range(PEER_TOPK):
            t1a = t1[a:a + 1]
            th_a = jnp.min(jnp.where(t1a + t2 >= tau, t2, jnp.inf), axis=0, keepdims=True)
            theta = jnp.where(s1 == t1a, th_a, theta)
        s2_ref[h] = s2
        a_ref[h] = jnp.exp(s1 - t1[0:1]) / zsum
        th_ref[h] = theta


def _peer_query(x1, wq_t, subkeys, tm):
    t, d = x1.shape
    nq = wq_t.shape[0]
    out = jax.ShapeDtypeStruct((PEER_HEADS, PEER_NKEYS, t), F32)
    ospec = pl.BlockSpec((PEER_HEADS, PEER_NKEYS, tm), lambda i: (0, 0, i))
    return pl.pallas_call(
        functools.partial(_peerq_kernel, tm=tm),
        grid=(t // tm,),
        in_specs=[
            pl.BlockSpec((tm, d), lambda i: (i, 0)),
            pl.BlockSpec((nq, d), lambda i: (0, 0)),
            pl.BlockSpec((2 * PEER_HEADS, PEER_NKEYS, PEER_HALF), lambda i: (0, 0, 0)),
        ],
        out_specs=[ospec, ospec, ospec],
        out_shape=[out, out, out],
        compiler_params=_cparams("parallel"),
        name="peer_query",
    )(x1, wq_t, subkeys)


def _peer_dense_kernel(x_ref, u_ref, vt_ref, s2_ref, a_ref, th_ref, g_ref, b_ref, o_ref,
                       xt_ref, bexp_ref, acc_ref, act_ref, *, te, alpha):
    e = pl.program_id(1)
    nblk = te // PEER_NKEYS

    @pl.when(e == 0)
    def _():
        xt_ref[...] = x_ref[...].T.astype(BF16)
        for h in range(PEER_HEADS):
            s2 = s2_ref[h]
            bexp_ref[h] = jnp.exp(s2 - jnp.max(s2, axis=0, keepdims=True))
        acc_ref[...] = jnp.zeros_like(acc_ref)

    ht = _dot(u_ref[...], xt_ref[...])
    for j in range(nblk):
        i1 = e * nblk + j
        hj = ht[j * PEER_NKEYS:(j + 1) * PEER_NKEYS]
        w = jnp.zeros_like(hj)
        for h in range(PEER_HEADS):
            th = th_ref[h, pl.ds(i1, 1), :]
            a = a_ref[h, pl.ds(i1, 1), :]
            w = w + jnp.where(s2_ref[h] >= th, bexp_ref[h], 0.0) * a
        act = 0.5 * hj * (1.0 + lax.erf(hj * np.float32(np.sqrt(0.5)))) * w
        act_ref[j * PEER_NKEYS:(j + 1) * PEER_NKEYS, :] = act.astype(BF16)
    acc_ref[...] += _dot(vt_ref[...], act_ref[...])

    @pl.when(e == pl.num_programs(1) - 1)
    def _():
        r = alpha * x_ref[...] + acc_ref[...].T
        o_ref[...] = _layer_norm(r, g_ref[...], b_ref[...])


def _peer_dense(x1, u, vt, s2, a, th, ln_g, ln_b, alpha, tm, te):
    t, d = x1.shape
    ne = u.shape[0]
    sel = pl.BlockSpec((PEER_HEADS, PEER_NKEYS, tm), lambda i, e: (0, 0, i))
    return pl.pallas_call(
        functools.partial(_peer_dense_kernel, te=te, alpha=alpha),
        grid=(t // tm, ne // te),
        in_specs=[
            pl.BlockSpec((tm, d), lambda i, e: (i, 0)),
            pl.BlockSpec((te, d), lambda i, e: (e, 0)),
            pl.BlockSpec((d, te), lambda i, e: (0, e)),
            sel, sel, sel,
            pl.BlockSpec((1, d), lambda i, e: (0, 0)),
            pl.BlockSpec((1, d), lambda i, e: (0, 0)),
        ],
        out_specs=pl.BlockSpec((tm, d), lambda i, e: (i, 0)),
        out_shape=jax.ShapeDtypeStruct((t, d), F32),
        scratch_shapes=[
            pltpu.VMEM((d, tm), BF16),
            pltpu.VMEM((PEER_HEADS, PEER_NKEYS, tm), F32),
            pltpu.VMEM((d, tm), F32),
            pltpu.VMEM((te, tm), BF16),
        ],
        compiler_params=_cparams("parallel", "arbitrary"),
        name="peer_dense",
    )(x1, u, vt, s2, a, th, ln_g, ln_b)


def _aug_weight():
    wa = np.zeros((LANES, 3 * FOX_HEADS * LANES), np.float32)
    kb = FOX_HEADS * LANES
    for h in range(FOX_HEADS):
        qc, kc, vc = h * LANES, kb + h * LANES, 2 * kb + h * LANES
        wa[AUG_HI + h, qc + 64] = 1.0
        wa[AUG_MID + h, qc + 65] = 1.0
        wa[AUG_LO + h, qc + 66] = 1.0
        wa[AUG_ONE, qc + 67:qc + 70] = 1.0
        wa[AUG_ONE, kc + 64:kc + 67] = 1.0
        wa[AUG_HI + h, kc + 67] = -1.0
        wa[AUG_MID + h, kc + 68] = -1.0
        wa[AUG_LO + h, kc + 69] = -1.0
        wa[AUG_ONE, vc + (64 if h % 2 == 0 else 0)] = 1.0
    return wa


def _split_w_in(w):
    d = D_MODEL
    fw = FOX_HEADS * FOX_HEAD_DIM
    gk = GLA_HEADS * GLA_HK
    gv = GLA_HEADS * GLA_HV
    sizes = (fw, fw, fw, FOX_HEADS, gk, gk, gv, GLA_GATE_RANK, gv, d, d)
    pts = np.cumsum(sizes)[:-1]
    return jnp.split(w, [int(p) for p in pts], axis=1)


def _fox_weight(fq, fk, fv):
    d = fq.shape[0]
    pad = FOX_HEAD_DIM

    def heads(w, scale):
        w = w.reshape(d, FOX_HEADS, FOX_HEAD_DIM) * scale
        return jnp.pad(w, ((0, 0), (0, 0), (0, pad))).reshape(d, FOX_HEADS * LANES)

    v = fv.reshape(d, FOX_HEADS // 2, 2, FOX_HEAD_DIM)
    v_even = jnp.pad(v[:, :, 0], ((0, 0), (0, 0), (0, pad)))
    v_odd = jnp.pad(v[:, :, 1], ((0, 0), (0, 0), (pad, 0)))
    wv = jnp.stack([v_even, v_odd], axis=2).reshape(d, FOX_HEADS * LANES)
    return jnp.concatenate([heads(fq, FOX_HEAD_DIM ** -0.5), heads(fk, 1.0), wv], axis=1)


def _pick(n, pref):
    return pref if n % pref == 0 else n


def kernel(x, w_in, b_fgate, w_gla_g2, b_gla_g2, gla_norm_g, w_br_fox, w_br_gla, w_o,
           ln1_g, ln1_b, peer_wq, peer_subkeys, peer_u, peer_v, ln2_g, ln2_b):
    batch, seq, d = x.shape
    depth = w_in.shape[0]
    t = batch * seq
    alpha = float((2.0 * depth) ** 0.25)
    w_aug = jnp.asarray(_aug_weight(), BF16)

    x2 = x.reshape(t, d)
    for l in range(depth):
        fq, fk, fv, ff, gq, gk, gv, gg1, gog, zf, zg = _split_w_in(w_in[l])
        w_fox = _fox_weight(fq, fk, fv).astype(BF16)
        w_rest = jnp.concatenate([gq, gk, gv, gog, zf, zg], axis=1).astype(BF16)
        w_small = jnp.pad(jnp.concatenate([ff, gg1], axis=1),
                          ((0, 0), (0, LANES - FOX_HEADS - GLA_GATE_RANK))).astype(BF16)
        bias_f = jnp.pad(b_fgate[l], (0, LANES - FOX_HEADS)).reshape(1, LANES)
        w_g2 = jnp.pad(w_gla_g2[l], ((FOX_HEADS, LANES - FOX_HEADS - GLA_GATE_RANK),
                                     (0, 0))).astype(BF16)
        b_g2 = b_gla_g2[l].reshape(1, -1)
        norm_g = gla_norm_g[l].reshape(1, -1)

        tm_proj = _pick(t, 1024)
        rest, small = _proj_rest(x2, w_rest, w_small, tm_proj, 512)
        xaug = _forget_cumsum(small, bias_f, batch, seq, _pick(seq, 512))
        qkv = _proj_fox(x2, xaug, w_fox, w_aug, tm_proj, 512)
        fox = _fox_attention(qkv, batch, seq, _pick(seq, 256))
        gla = _gla(rest, small, w_g2, b_g2, norm_g, batch, seq, _pick(seq, 256), 64)
        x1 = _merge(x2, fox, gla, rest, w_br_fox[l].astype(BF16), w_br_gla[l].astype(BF16),
                    w_o[l].astype(BF16), ln1_g[l].reshape(1, d), ln1_b[l].reshape(1, d),
                    alpha, _pick(t, 512))

        wq_t = peer_wq[l].T.astype(BF16)
        subkeys = peer_subkeys[l].reshape(2 * PEER_HEADS, PEER_NKEYS, PEER_HALF).astype(BF16)
        s2, a, th = _peer_query(x1, wq_t, subkeys, _pick(t, 256))
        x2 = _peer_dense(x1, peer_u[l].astype(BF16), peer_v[l].T.astype(BF16), s2, a, th,
                         ln2_g[l].reshape(1, d), ln2_b[l].reshape(1, d), alpha,
                         _pick(t, 512), 512)
    return x2.reshape(batch, seq, d)
```

```python
import functools

import numpy as np
import jax
import jax.numpy as jnp
from jax import lax
from jax.experimental import pallas as pl
from jax.experimental.pallas import tpu as pltpu

F32 = jnp.float32
BF16 = jnp.bfloat16

D_MODEL = 1024
FOX_HEADS = 16
FOX_HEAD_DIM = 64
GLA_HEADS = 4
GLA_HK = 128
GLA_HV = 256
GLA_GATE_RANK = 16
GLA_GATE_NORMALIZER = 16.0
PEER_HEADS = 8
PEER_NKEYS = 128
PEER_HALF = 128
PEER_TOPK = 16
LN_EPS = 1e-5
RMS_EPS = 1e-6

LANES = 128
VMEM_LIMIT = 56 * 1024 * 1024

AUG_HI, AUG_MID, AUG_LO, AUG_ONE = 0, 16, 32, 48
NEG_BIG = -1e30


def _cparams(*sem):
    return pltpu.CompilerParams(dimension_semantics=sem, vmem_limit_bytes=VMEM_LIMIT)


def _log_sigmoid(z):
    return jnp.minimum(z, 0.0) - jnp.log1p(jnp.exp(-jnp.abs(z)))


def _layer_norm(r, g, b):
    mu = jnp.mean(r, axis=-1, keepdims=True)
    d = r - mu
    var = jnp.mean(d * d, axis=-1, keepdims=True)
    return d * lax.rsqrt(var + LN_EPS) * g + b


def _dot(a, b):
    return jnp.dot(a, b, preferred_element_type=F32)


def _dot_nt(a, b):
    return lax.dot_general(a, b, (((1,), (1,)), ((), ())), preferred_element_type=F32)


def _dot_tn(a, b):
    return lax.dot_general(a, b, (((0,), (0,)), ((), ())), preferred_element_type=F32)


def _proj_rest_kernel(x_ref, w_ref, ws_ref, o_ref, os_ref):
    xb = x_ref[...].astype(BF16)
    o_ref[...] = _dot(xb, w_ref[...]).astype(o_ref.dtype)

    @pl.when(pl.program_id(1) == 0)
    def _():
        os_ref[...] = _dot(xb, ws_ref[...])


def _proj_fox_kernel(x_ref, xa_ref, w_ref, wa_ref, o_ref):
    acc = _dot(x_ref[...].astype(BF16), w_ref[...]) + _dot(xa_ref[...], wa_ref[...])
    o_ref[...] = acc.astype(o_ref.dtype)


def _proj_rest(x2, w_rest, w_small, tm, tn):
    t, d = x2.shape
    n = w_rest.shape[1]
    return pl.pallas_call(
        _proj_rest_kernel,
        grid=(t // tm, n // tn),
        in_specs=[
            pl.BlockSpec((tm, d), lambda i, j: (i, 0)),
            pl.BlockSpec((d, tn), lambda i, j: (0, j)),
            pl.BlockSpec((d, LANES), lambda i, j: (0, 0)),
        ],
        out_specs=[
            pl.BlockSpec((tm, tn), lambda i, j: (i, j)),
            pl.BlockSpec((tm, LANES), lambda i, j: (i, 0)),
        ],
        out_shape=[
            jax.ShapeDtypeStruct((t, n), BF16),
            jax.ShapeDtypeStruct((t, LANES), F32),
        ],
        compiler_params=_cparams("parallel", "arbitrary"),
        name="proj_rest",
    )(x2, w_rest, w_small)


def _proj_fox(x2, xaug, w_fox, w_aug, tm, tn):
    t, d = x2.shape
    n = w_fox.shape[1]
    return pl.pallas_call(
        _proj_fox_kernel,
        grid=(t // tm, n // tn),
        in_specs=[
            pl.BlockSpec((tm, d), lambda i, j: (i, 0)),
            pl.BlockSpec((tm, LANES), lambda i, j: (i, 0)),
            pl.BlockSpec((d, tn), lambda i, j: (0, j)),
            pl.BlockSpec((LANES, tn), lambda i, j: (0, j)),
        ],
        out_specs=pl.BlockSpec((tm, tn), lambda i, j: (i, j)),
        out_shape=jax.ShapeDtypeStruct((t, n), BF16),
        compiler_params=_cparams("parallel", "arbitrary"),
        name="proj_fox",
    )(x2, xaug, w_fox, w_aug)


def _cumsum_kernel(s_ref, bias_ref, o_ref, carry_ref, *, tc):
    @pl.when(pl.program_id(1) == 0)
    def _():
        carry_ref[...] = jnp.zeros_like(carry_ref)

    z = s_ref[...] + bias_ref[...]
    lane = lax.broadcasted_iota(jnp.int32, z.shape, 1)
    logf = jnp.where(lane < FOX_HEADS, _log_sigmoid(z), 0.0)
    row = lax.broadcasted_iota(jnp.int32, (tc, tc), 0)
    col = lax.broadcasted_iota(jnp.int32, (tc, tc), 1)
    ltri = (row >= col).astype(F32)
    c = jnp.dot(ltri, logf, precision=lax.Precision.HIGHEST,
                preferred_element_type=F32) + carry_ref[0:1, :]
    carry_ref[...] = jnp.broadcast_to(c[tc - 1:tc, :], carry_ref.shape)
    c_mid = pltpu.roll(c, AUG_MID, 1)
    c_lo = pltpu.roll(c, AUG_LO, 1)
    csel = jnp.where(lane < AUG_MID, c, jnp.where(lane < AUG_LO, c_mid, c_lo))
    hi = csel.astype(BF16).astype(F32)
    r1 = csel - hi
    mid = r1.astype(BF16).astype(F32)
    lo = (r1 - mid).astype(BF16).astype(F32)
    val = jnp.where(lane < AUG_MID, hi,
                    jnp.where(lane < AUG_LO, mid,
                              jnp.where(lane < AUG_ONE, lo,
                                        jnp.where(lane == AUG_ONE, 1.0, 0.0))))
    o_ref[...] = val.astype(o_ref.dtype)


def _forget_cumsum(small, bias, batch, seq, tc):
    t = small.shape[0]
    nb = seq // tc
    return pl.pallas_call(
        functools.partial(_cumsum_kernel, tc=tc),
        grid=(batch, nb),
        in_specs=[
            pl.BlockSpec((tc, LANES), lambda b, s: (b * nb + s, 0)),
            pl.BlockSpec((1, LANES), lambda b, s: (0, 0)),
        ],
        out_specs=pl.BlockSpec((tc, LANES), lambda b, s: (b * nb + s, 0)),
        out_shape=jax.ShapeDtypeStruct((t, LANES), BF16),
        scratch_shapes=[pltpu.VMEM((8, LANES), F32)],
        compiler_params=_cparams("parallel", "arbitrary"),
        name="forget_cumsum",
    )(small, bias)


def _fox_kernel(q_ref, k_ref, v_ref, o_ref, m_ref, acc_ref, *, tq):
    qi = pl.program_id(2)
    lane = lax.broadcasted_iota(jnp.int32, (tq, LANES), 1)
    row = lax.broadcasted_iota(jnp.int32, (tq, tq), 0)
    col = lax.broadcasted_iota(jnp.int32, (tq, tq), 1)
    causal = row >= col
    m_ref[...] = jnp.full(m_ref.shape, NEG_BIG, F32)
    acc_ref[...] = jnp.zeros(acc_ref.shape, F32)

    def step(kb, masked):
        start = pl.multiple_of(kb * tq, tq)
        for hh in range(2):
            lanes = slice(hh * LANES, (hh + 1) * LANES)
            k = k_ref[pl.ds(start, tq), lanes]
            v = v_ref[pl.ds(start, tq), lanes]
            s = _dot_nt(q_ref[:, lanes], k)
            if masked:
                s = jnp.where(causal, s, NEG_BIG)
            m_prev = m_ref[hh]
            m_next = jnp.maximum(m_prev, jnp.max(s, axis=1, keepdims=True))
            p = jnp.exp(s - jnp.tile(m_next, (1, tq // LANES)))
            alpha = jnp.exp(m_prev - m_next)
            acc_ref[hh] = acc_ref[hh] * alpha + _dot(p.astype(BF16), v)
            m_ref[hh] = m_next

    def body(kb, carry):
        step(kb, False)
        return carry

    lax.fori_loop(0, qi, body, 0)
    step(qi, True)
    outs = []
    for hh in range(2):
        acc = acc_ref[hh]
        l_lane = 64 if hh == 0 else 0
        l = jnp.sum(jnp.where(lane == l_lane, acc, 0.0), axis=1, keepdims=True)
        outs.append(acc / l)
    o_ref[...] = jnp.where(lane < 64, outs[0], outs[1]).astype(o_ref.dtype)


def _fox_attention(qkv, batch, seq, tq):
    t = qkv.shape[0]
    nq = seq // tq
    npair = FOX_HEADS // 2
    return pl.pallas_call(
        functools.partial(_fox_kernel, tq=tq),
        grid=(batch, npair, nq),
        in_specs=[
            pl.BlockSpec((tq, 2 * LANES), lambda b, h, i: (b * nq + i, h)),
            pl.BlockSpec((seq, 2 * LANES), lambda b, h, i: (b, npair + h)),
            pl.BlockSpec((seq, 2 * LANES), lambda b, h, i: (b, 2 * npair + h)),
        ],
        out_specs=pl.BlockSpec((tq, LANES), lambda b, h, i: (b * nq + i, h)),
        out_shape=jax.ShapeDtypeStruct((t, FOX_HEADS * FOX_HEAD_DIM), BF16),
        scratch_shapes=[pltpu.VMEM((2, tq, LANES), F32), pltpu.VMEM((2, tq, LANES), F32)],
        compiler_params=_cparams("parallel", "parallel", "arbitrary"),
        name="fox_attention",
    )(qkv, qkv, qkv)


def _gla_kernel(q_ref, k_ref, v_ref, sm_ref, gog_ref, wg_ref, bg_ref, ng_ref, o_ref,
                st_ref, *, rows, chunk):
    @pl.when(pl.program_id(2) == 0)
    def _():
        st_ref[...] = jnp.zeros_like(st_ref)

    z = _dot(sm_ref[...].astype(BF16), wg_ref[...]) + bg_ref[...]
    log_a = _log_sigmoid(z) * (1.0 / GLA_GATE_NORMALIZER)
    ri = lax.broadcasted_iota(jnp.int32, (chunk, chunk), 0)
    ci = lax.broadcasted_iota(jnp.int32, (chunk, chunk), 1)
    tril = ri >= ci
    ltri = tril.astype(F32)
    for c in range(rows // chunk):
        sl = slice(c * chunk, (c + 1) * chunk)
        bc = jnp.dot(ltri, log_a[sl], precision=lax.Precision.HIGHEST,
                     preferred_element_type=F32)
        b_last = bc[chunk - 1:chunk, :]
        kf = k_ref[sl, :].astype(F32)
        qe = (q_ref[sl, :].astype(F32) * (GLA_HK ** -0.5) * jnp.exp(bc)).astype(BF16)
        kd = (kf * jnp.exp(-bc)).astype(BF16)
        kdec = (kf * jnp.exp(b_last - bc)).astype(BF16)
        v = v_ref[sl, :]
        attn = jnp.where(tril, _dot_nt(qe, kd), 0.0)
        st = st_ref[...]
        o = _dot(attn.astype(BF16), v) + _dot_nt(qe, st.astype(BF16))
        st_ref[...] = st * jnp.exp(b_last) + _dot_tn(v, kdec)
        ms = jnp.mean(o * o, axis=-1, keepdims=True)
        gate = gog_ref[sl, :].astype(F32)
        y = o * lax.rsqrt(ms + RMS_EPS) * ng_ref[...] * (gate * jax.nn.sigmoid(gate))
        o_ref[sl, :] = y.astype(o_ref.dtype)


def _gla(rest, small, w_g2, b_g2, norm_g, batch, seq, rows, chunk):
    t = rest.shape[0]
    nr = seq // rows
    kq = 0
    kk = GLA_HEADS * GLA_HK // GLA_HK
    kv = 2 * GLA_HEADS * GLA_HK // GLA_HV
    kg = kv + GLA_HEADS
    return pl.pallas_call(
        functools.partial(_gla_kernel, rows=rows, chunk=chunk),
        grid=(batch, GLA_HEADS, nr),
        in_specs=[
            pl.BlockSpec((rows, GLA_HK), lambda b, h, s: (b * nr + s, kq + h)),
            pl.BlockSpec((rows, GLA_HK), lambda b, h, s: (b * nr + s, kk + h)),
            pl.BlockSpec((rows, GLA_HV), lambda b, h, s: (b * nr + s, kv + h)),
            pl.BlockSpec((rows, LANES), lambda b, h, s: (b * nr + s, 0)),
            pl.BlockSpec((rows, GLA_HV), lambda b, h, s: (b * nr + s, kg + h)),
            pl.BlockSpec((LANES, GLA_HK), lambda b, h, s: (0, h)),
            pl.BlockSpec((1, GLA_HK), lambda b, h, s: (0, h)),
            pl.BlockSpec((1, GLA_HV), lambda b, h, s: (0, 0)),
        ],
        out_specs=pl.BlockSpec((rows, GLA_HV), lambda b, h, s: (b * nr + s, h)),
        out_shape=jax.ShapeDtypeStruct((t, GLA_HEADS * GLA_HV), BF16),
        scratch_shapes=[pltpu.VMEM((GLA_HV, GLA_HK), F32)],
        compiler_params=_cparams("parallel", "parallel", "arbitrary"),
        name="gla",
    )(rest, rest, rest, small, rest, w_g2, b_g2, norm_g)


def _merge_kernel(x_ref, fox_ref, gla_ref, zf_ref, zg_ref, wf_ref, wg_ref, wo_ref,
                  g_ref, b_ref, o_ref, *, alpha):
    a = _dot(fox_ref[...], wf_ref[...])
    b = _dot(gla_ref[...], wg_ref[...])
    merged = (jax.nn.sigmoid(zf_ref[...].astype(F32)) * a
              + jax.nn.sigmoid(zg_ref[...].astype(F32)) * b)
    y = _dot(merged.astype(BF16), wo_ref[...])
    o_ref[...] = _layer_norm(alpha * x_ref[...] + y, g_ref[...], b_ref[...])


def _merge(x2, fox, gla, rest, wf, wg, wo, ln_g, ln_b, alpha, tm):
    t, d = x2.shape
    zf_blk = (rest.shape[1] - 2 * d) // d
    full = lambda i: (0, 0)
    return pl.pallas_call(
        functools.partial(_merge_kernel, alpha=alpha),
        grid=(t // tm,),
        in_specs=[
            pl.BlockSpec((tm, d), lambda i: (i, 0)),
            pl.BlockSpec((tm, d), lambda i: (i, 0)),
            pl.BlockSpec((tm, d), lambda i: (i, 0)),
            pl.BlockSpec((tm, d), lambda i: (i, zf_blk)),
            pl.BlockSpec((tm, d), lambda i: (i, zf_blk + 1)),
            pl.BlockSpec((d, d), full),
            pl.BlockSpec((d, d), full),
            pl.BlockSpec((d, d), full),
            pl.BlockSpec((1, d), full),
            pl.BlockSpec((1, d), full),
        ],
        out_specs=pl.BlockSpec((tm, d), lambda i: (i, 0)),
        out_shape=jax.ShapeDtypeStruct((t, d), F32),
        compiler_params=_cparams("parallel"),
        name="merge_ln",
    )(x2, fox, gla, rest, rest, wf, wg, wo, ln_g, ln_b)


def _top_rows(s, k):
    tm = s.shape[1]
    row = lax.broadcasted_iota(jnp.int32, (k, tm), 0)
    out = jnp.full((k, tm), -jnp.inf, F32)
    cur = s
    for r in range(k):
        m = jnp.max(cur, axis=0, keepdims=True)
        out = jnp.where(row == r, m, out)
        cur = jnp.where(cur == m, -jnp.inf, cur)
    return out


def _peerq_kernel(x_ref, wq_ref, sk_ref, s2_ref, a_ref, th_ref, *, tm):
    xb = x_ref[...].astype(BF16)
    qt = _dot_nt(wq_ref[...], xb)
    row8 = lax.broadcasted_iota(jnp.int32, (8, tm), 0)
    for h in range(PEER_HEADS):
        q1 = qt[(2 * h) * PEER_HALF:(2 * h + 1) * PEER_HALF].astype(BF16)
        q2 = qt[(2 * h + 1) * PEER_HALF:(2 * h + 2) * PEER_HALF].astype(BF16)
        s1 = _dot(sk_ref[2 * h], q1)
        s2 = _dot(sk_ref[2 * h + 1], q2)
        t1 = _top_rows(s1, PEER_TOPK)
        t2 = _top_rows(s2, PEER_TOPK)
        blocks = [t1[0:1] + t2]
        for a in range(1, 8):
            nb = PEER_TOPK // (a + 1)
            blocks.append(jnp.where(row8 < nb, t1[a:a + 1] + t2[0:8], -jnp.inf))
        blocks.append(t1[8:16] + t2[0:1])
        cand = jnp.concatenate(blocks, axis=0)
        best = _top_rows(cand, PEER_TOPK)
        tau = best[PEER_TOPK - 1:PEER_TOPK]
        zsum = jnp.sum(jnp.exp(best - best[0:1]), axis=0, keepdims=True)
        theta = jnp.full(s1.shape, jnp.inf, F32)
        for a in range(PEER_TOPK):
            t1a = t1[a:a + 1]
            th_a = jnp.min(jnp.where(t1a + t2 >= tau, t2, jnp.inf), axis=0, keepdims=True)
            theta = jnp.where(s1 == t1a, th_a, theta)
        s2_ref[h] = s2
        a_ref[h] = jnp.exp(s1 - t1[0:1]) / zsum
        th_ref[h] = theta


def _peer_query(x1, wq_t, subkeys, tm):
    t, d = x1.shape
    nq = wq_t.shape[0]
    out = jax.ShapeDtypeStruct((PEER_HEADS, PEER_NKEYS, t), F32)
    ospec = pl.BlockSpec((PEER_HEADS, PEER_NKEYS, tm), lambda i: (0, 0, i))
    return pl.pallas_call(
        functools.partial(_peerq_kernel, tm=tm),
        grid=(t // tm,),
        in_specs=[
            pl.BlockSpec((tm, d), lambda i: (i, 0)),
            pl.BlockSpec((nq, d), lambda i: (0, 0)),
            pl.BlockSpec((2 * PEER_HEADS, PEER_NKEYS, PEER_HALF), lambda i: (0, 0, 0)),
        ],
        out_specs=[ospec, ospec, ospec],
        out_shape=[out, out, out],
        compiler_params=_cparams("parallel"),
        name="peer_query",
    )(x1, wq_t, subkeys)


def _peer_dense_kernel(x_ref, u_ref, vt_ref, s2_ref, a_ref, th_ref, g_ref, b_ref, o_ref,
                       xt_ref, bexp_ref, acc_ref, act_ref, *, te, alpha):
    e = pl.program_id(1)
    nblk = te // PEER_NKEYS

    @pl.when(e == 0)
    def _():
        xt_ref[...] = x_ref[...].T.astype(BF16)
        for h in range(PEER_HEADS):
            s2 = s2_ref[h]
            bexp_ref[h] = jnp.exp(s2 - jnp.max(s2, axis=0, keepdims=True))
        acc_ref[...] = jnp.zeros_like(acc_ref)

    ht = _dot(u_ref[...], xt_ref[...])
    for j in range(nblk):
        i1 = e * nblk + j
        hj = ht[j * PEER_NKEYS:(j + 1) * PEER_NKEYS]
        w = jnp.zeros_like(hj)
        for h in range(PEER_HEADS):
            th = th_ref[h, pl.ds(i1, 1), :]
            a = a_ref[h, pl.ds(i1, 1), :]
            w = w + jnp.where(s2_ref[h] >= th, bexp_ref[h], 0.0) * a
        act = 0.5 * hj * (1.0 + lax.erf(hj * np.float32(np.sqrt(0.5)))) * w
        act_ref[j * PEER_NKEYS:(j + 1) * PEER_NKEYS, :] = act.astype(BF16)
    acc_ref[...] += _dot(vt_ref[...], act_ref[...])

    @pl.when(e == pl.num_programs(1) - 1)
    def _():
        r = alpha * x_ref[...] + acc_ref[...].T
        o_ref[...] = _layer_norm(r, g_ref[...], b_ref[...])


def _peer_dense(x1, u, vt, s2, a, th, ln_g, ln_b, alpha, tm, te):
    t, d = x1.shape
    ne = u.shape[0]
    sel = pl.BlockSpec((PEER_HEADS, PEER_NKEYS, tm), lambda i, e: (0, 0, i))
    return pl.pallas_call(
        functools.partial(_peer_dense_kernel, te=te, alpha=alpha),
        grid=(t // tm, ne // te),
        in_specs=[
            pl.BlockSpec((tm, d), lambda i, e: (i, 0)),
            pl.BlockSpec((te, d), lambda i, e: (e, 0)),
            pl.BlockSpec((d, te), lambda i, e: (0, e)),
            sel, sel, sel,
            pl.BlockSpec((1, d), lambda i, e: (0, 0)),
            pl.BlockSpec((1, d), lambda i, e: (0, 0)),
        ],
        out_specs=pl.BlockSpec((tm, d), lambda i, e: (i, 0)),
        out_shape=jax.ShapeDtypeStruct((t, d), F32),
        scratch_shapes=[
            pltpu.VMEM((d, tm), BF16),
            pltpu.VMEM((PEER_HEADS, PEER_NKEYS, tm), F32),
            pltpu.VMEM((d, tm), F32),
            pltpu.VMEM((te, tm), BF16),
        ],
        compiler_params=_cparams("parallel", "arbitrary"),
        name="peer_dense",
    )(x1, u, vt, s2, a, th, ln_g, ln_b)


def _aug_weight():
    wa = np.zeros((LANES, 3 * FOX_HEADS * LANES), np.float32)
    kb = FOX_HEADS * LANES
    for h in range(FOX_HEADS):
        qc, kc, vc = h * LANES, kb + h * LANES, 2 * kb + h * LANES
        wa[AUG_HI + h, qc + 64] = 1.0
        wa[AUG_MID + h, qc + 65] = 1.0
        wa[AUG_LO + h, qc + 66] = 1.0
        wa[AUG_ONE, qc + 67:qc + 70] = 1.0
        wa[AUG_ONE, kc + 64:kc + 67] = 1.0
        wa[AUG_HI + h, kc + 67] = -1.0
        wa[AUG_MID + h, kc + 68] = -1.0
        wa[AUG_LO + h, kc + 69] = -1.0
        wa[AUG_ONE, vc + (64 if h % 2 == 0 else 0)] = 1.0
    return wa


def _split_w_in(w):
    d = D_MODEL
    fw = FOX_HEADS * FOX_HEAD_DIM
    gk = GLA_HEADS * GLA_HK
    gv = GLA_HEADS * GLA_HV
    sizes = (fw, fw, fw, FOX_HEADS, gk, gk, gv, GLA_GATE_RANK, gv, d, d)
    pts = np.cumsum(sizes)[:-1]
    return jnp.split(w, [int(p) for p in pts], axis=1)


def _fox_weight(fq, fk, fv):
    d = fq.shape[0]
    pad = FOX_HEAD_DIM

    def heads(w, scale):
        w = w.reshape(d, FOX_HEADS, FOX_HEAD_DIM) * scale
        return jnp.pad(w, ((0, 0), (0, 0), (0, pad))).reshape(d, FOX_HEADS * LANES)

    v = fv.reshape(d, FOX_HEADS // 2, 2, FOX_HEAD_DIM)
    v_even = jnp.pad(v[:, :, 0], ((0, 0), (0, 0), (0, pad)))
    v_odd = jnp.pad(v[:, :, 1], ((0, 0), (0, 0), (pad, 0)))
    wv = jnp.stack([v_even, v_odd], axis=2).reshape(d, FOX_HEADS * LANES)
    return jnp.concatenate([heads(fq, FOX_HEAD_DIM ** -0.5), heads(fk, 1.0), wv], axis=1)


def _pick(n, pref):
    return pref if n % pref == 0 else n


def kernel(x, w_in, b_fgate, w_gla_g2, b_gla_g2, gla_norm_g, w_br_fox, w_br_gla, w_o,
           ln1_g, ln1_b, peer_wq, peer_subkeys, peer_u, peer_v, ln2_g, ln2_b):
    batch, seq, d = x.shape
    depth = w_in.shape[0]
    t = batch * seq
    alpha = float((2.0 * depth) ** 0.25)
    w_aug = jnp.asarray(_aug_weight(), BF16)

    x2 = x.reshape(t, d)
    for l in range(depth):
        fq, fk, fv, ff, gq, gk, gv, gg1, gog, zf, zg = _split_w_in(w_in[l])
        w_fox = _fox_weight(fq, fk, fv).astype(BF16)
        w_rest = jnp.concatenate([gq, gk, gv, gog, zf, zg], axis=1).astype(BF16)
        w_small = jnp.pad(jnp.concatenate([ff, gg1], axis=1),
                          ((0, 0), (0, LANES - FOX_HEADS - GLA_GATE_RANK))).astype(BF16)
        bias_f = jnp.pad(b_fgate[l], (0, LANES - FOX_HEADS)).reshape(1, LANES)
        w_g2 = jnp.pad(w_gla_g2[l], ((FOX_HEADS, LANES - FOX_HEADS - GLA_GATE_RANK),
                                     (0, 0))).astype(BF16)
        b_g2 = b_gla_g2[l].reshape(1, -1)
        norm_g = gla_norm_g[l].reshape(1, -1)

        tm_proj = _pick(t, 1024)
        rest, small = _proj_rest(x2, w_rest, w_small, tm_proj, 512)
        xaug = _forget_cumsum(small, bias_f, batch, seq, _pick(seq, 512))
        qkv = _proj_fox(x2, xaug, w_fox, w_aug, tm_proj, 512)
        fox = _fox_attention(qkv, batch, seq, _pick(seq, 512))
        gla = _gla(rest, small, w_g2, b_g2, norm_g, batch, seq, _pick(seq, 256), 64)
        x1 = _merge(x2, fox, gla, rest, w_br_fox[l].astype(BF16), w_br_gla[l].astype(BF16),
                    w_o[l].astype(BF16), ln1_g[l].reshape(1, d), ln1_b[l].reshape(1, d),
                    alpha, _pick(t, 512))

        wq_t = peer_wq[l].T.astype(BF16)
        subkeys = peer_subkeys[l].reshape(2 * PEER_HEADS, PEER_NKEYS, PEER_HALF).astype(BF16)
        s2, a, th = _peer_query(x1, wq_t, subkeys, _pick(t, 256))
        x2 = _peer_dense(x1, peer_u[l].astype(BF16), peer_v[l].T.astype(BF16), s2, a, th,
                         ln2_g[l].reshape(1, d), ln2_b[l].reshape(1, d), alpha,
                         _pick(t, 512), 512)
    return x2.reshape(batch, seq, d)
```

```python
import functools

import numpy as np
import jax
import jax.numpy as jnp
from jax import lax
from jax.experimental import pallas as pl
from jax.experimental.pallas import tpu as pltpu

F32 = jnp.float32
BF16 = jnp.bfloat16

D_MODEL = 1024
FOX_HEADS = 16
FOX_HEAD_DIM = 64
GLA_HEADS = 4
GLA_HK = 128
GLA_HV = 256
GLA_GATE_RANK = 16
GLA_GATE_NORMALIZER = 16.0
PEER_HEADS = 8
PEER_NKEYS = 128
PEER_HALF = 128
PEER_TOPK = 16
LN_EPS = 1e-5
RMS_EPS = 1e-6

LANES = 128
VMEM_LIMIT = 56 * 1024 * 1024

AUG_HI, AUG_MID, AUG_LO, AUG_ONE = 0, 16, 32, 48
NEG_BIG = -1e30


def _cparams(*sem):
    return pltpu.CompilerParams(dimension_semantics=sem, vmem_limit_bytes=VMEM_LIMIT)


def _log_sigmoid(z):
    return jnp.minimum(z, 0.0) - jnp.log1p(jnp.exp(-jnp.abs(z)))


def _layer_norm(r, g, b):
    mu = jnp.mean(r, axis=-1, keepdims=True)
    d = r - mu
    var = jnp.mean(d * d, axis=-1, keepdims=True)
    return d * lax.rsqrt(var + LN_EPS) * g + b


def _dot(a, b):
    return jnp.dot(a, b, preferred_element_type=F32)


def _dot_nt(a, b):
    return lax.dot_general(a, b, (((1,), (1,)), ((), ())), preferred_element_type=F32)


def _dot_tn(a, b):
    return lax.dot_general(a, b, (((0,), (0,)), ((), ())), preferred_element_type=F32)


def _proj_rest_kernel(x_ref, w_ref, ws_ref, o_ref, os_ref):
    xb = x_ref[...].astype(BF16)
    o_ref[...] = _dot(xb, w_ref[...]).astype(o_ref.dtype)

    @pl.when(pl.program_id(1) == 0)
    def _():
        os_ref[...] = _dot(xb, ws_ref[...])


def _proj_fox_kernel(x_ref, xa_ref, w_ref, wa_ref, o_ref):
    acc = _dot(x_ref[...].astype(BF16), w_ref[...]) + _dot(xa_ref[...], wa_ref[...])
    o_ref[...] = acc.astype(o_ref.dtype)


def _proj_rest(x2, w_rest, w_small, tm, tn):
    t, d = x2.shape
    n = w_rest.shape[1]
    return pl.pallas_call(
        _proj_rest_kernel,
        grid=(t // tm, n // tn),
        in_specs=[
            pl.BlockSpec((tm, d), lambda i, j: (i, 0)),
            pl.BlockSpec((d, tn), lambda i, j: (0, j)),
            pl.BlockSpec((d, LANES), lambda i, j: (0, 0)),
        ],
        out_specs=[
            pl.BlockSpec((tm, tn), lambda i, j: (i, j)),
            pl.BlockSpec((tm, LANES), lambda i, j: (i, 0)),
        ],
        out_shape=[
            jax.ShapeDtypeStruct((t, n), BF16),
            jax.ShapeDtypeStruct((t, LANES), F32),
        ],
        compiler_params=_cparams("parallel", "arbitrary"),
        name="proj_rest",
    )(x2, w_rest, w_small)


def _proj_fox(x2, xaug, w_fox, w_aug, tm, tn):
    t, d = x2.shape
    n = w_fox.shape[1]
    return pl.pallas_call(
        _proj_fox_kernel,
        grid=(t // tm, n // tn),
        in_specs=[
            pl.BlockSpec((tm, d), lambda i, j: (i, 0)),
            pl.BlockSpec((tm, LANES), lambda i, j: (i, 0)),
            pl.BlockSpec((d, tn), lambda i, j: (0, j)),
            pl.BlockSpec((LANES, tn), lambda i, j: (0, j)),
        ],
        out_specs=pl.BlockSpec((tm, tn), lambda i, j: (i, j)),
        out_shape=jax.ShapeDtypeStruct((t, n), BF16),
        compiler_params=_cparams("parallel", "arbitrary"),
        name="proj_fox",
    )(x2, xaug, w_fox, w_aug)


def _cumsum_kernel(s_ref, bias_ref, o_ref, carry_ref, *, tc):
    @pl.when(pl.program_id(1) == 0)
    def _():
        carry_ref[...] = jnp.zeros_like(carry_ref)

    z = s_ref[...] + bias_ref[...]
    lane = lax.broadcasted_iota(jnp.int32, z.shape, 1)
    logf = jnp.where(lane < FOX_HEADS, _log_sigmoid(z), 0.0)
    row = lax.broadcasted_iota(jnp.int32, (tc, tc), 0)
    col = lax.broadcasted_iota(jnp.int32, (tc, tc), 1)
    ltri = (row >= col).astype(F32)
    c = jnp.dot(ltri, logf, precision=lax.Precision.HIGHEST,
                preferred_element_type=F32) + carry_ref[0:1, :]
    carry_ref[...] = jnp.broadcast_to(c[tc - 1:tc, :], carry_ref.shape)
    c_mid = pltpu.roll(c, AUG_MID, 1)
    c_lo = pltpu.roll(c, AUG_LO, 1)
    csel = jnp.where(lane < AUG_MID, c, jnp.where(lane < AUG_LO, c_mid, c_lo))
    hi = csel.astype(BF16).astype(F32)
    r1 = csel - hi
    mid = r1.astype(BF16).astype(F32)
    lo = (r1 - mid).astype(BF16).astype(F32)
    val = jnp.where(lane < AUG_MID, hi,
                    jnp.where(lane < AUG_LO, mid,
                              jnp.where(lane < AUG_ONE, lo,
                                        jnp.where(lane == AUG_ONE, 1.0, 0.0))))
    o_ref[...] = val.astype(o_ref.dtype)


def _forget_cumsum(small, bias, batch, seq, tc):
    t = small.shape[0]
    nb = seq // tc
    return pl.pallas_call(
        functools.partial(_cumsum_kernel, tc=tc),
        grid=(batch, nb),
        in_specs=[
            pl.BlockSpec((tc, LANES), lambda b, s: (b * nb + s, 0)),
            pl.BlockSpec((1, LANES), lambda b, s: (0, 0)),
        ],
        out_specs=pl.BlockSpec((tc, LANES), lambda b, s: (b * nb + s, 0)),
        out_shape=jax.ShapeDtypeStruct((t, LANES), BF16),
        scratch_shapes=[pltpu.VMEM((8, LANES), F32)],
        compiler_params=_cparams("parallel", "arbitrary"),
        name="forget_cumsum",
    )(small, bias)


def _fox_kernel(q_ref, k_ref, v_ref, o_ref, m_ref, acc_ref, *, tq):
    qi = pl.program_id(2)
    lane = lax.broadcasted_iota(jnp.int32, (tq, LANES), 1)
    row = lax.broadcasted_iota(jnp.int32, (tq, tq), 0)
    col = lax.broadcasted_iota(jnp.int32, (tq, tq), 1)
    causal = row >= col
    m_ref[...] = jnp.full(m_ref.shape, NEG_BIG, F32)
    acc_ref[...] = jnp.zeros(acc_ref.shape, F32)

    def step(kb, masked):
        start = pl.multiple_of(kb * tq, tq)
        for hh in range(2):
            lanes = slice(hh * LANES, (hh + 1) * LANES)
            k = k_ref[pl.ds(start, tq), lanes]
            v = v_ref[pl.ds(start, tq), lanes]
            s = _dot_nt(q_ref[:, lanes], k)
            if masked:
                s = jnp.where(causal, s, NEG_BIG)
            m_prev = m_ref[hh]
            m_next = jnp.maximum(m_prev, jnp.max(s, axis=1, keepdims=True))
            p = jnp.exp(s - jnp.tile(m_next, (1, tq // LANES)))
            alpha = jnp.exp(m_prev - m_next)
            acc_ref[hh] = acc_ref[hh] * alpha + _dot(p.astype(BF16), v)
            m_ref[hh] = m_next

    def body(kb, carry):
        step(kb, False)
        return carry

    lax.fori_loop(0, qi, body, 0)
    step(qi, True)
    outs = []
    for hh in range(2):
        acc = acc_ref[hh]
        l_lane = 64 if hh == 0 else 0
        l = jnp.sum(jnp.where(lane == l_lane, acc, 0.0), axis=1, keepdims=True)
        outs.append(acc / l)
    o_ref[...] = jnp.where(lane < 64, outs[0], outs[1]).astype(o_ref.dtype)


def _fox_attention(qkv, batch, seq, tq):
    t = qkv.shape[0]
    nq = seq // tq
    npair = FOX_HEADS // 2
    return pl.pallas_call(
        functools.partial(_fox_kernel, tq=tq),
        grid=(batch, npair, nq),
        in_specs=[
            pl.BlockSpec((tq, 2 * LANES), lambda b, h, i: (b * nq + i, h)),
            pl.BlockSpec((seq, 2 * LANES), lambda b, h, i: (b, npair + h)),
            pl.BlockSpec((seq, 2 * LANES), lambda b, h, i: (b, 2 * npair + h)),
        ],
        out_specs=pl.BlockSpec((tq, LANES), lambda b, h, i: (b * nq + i, h)),
        out_shape=jax.ShapeDtypeStruct((t, FOX_HEADS * FOX_HEAD_DIM), BF16),
        scratch_shapes=[pltpu.VMEM((2, tq, LANES), F32), pltpu.VMEM((2, tq, LANES), F32)],
        compiler_params=_cparams("parallel", "parallel", "arbitrary"),
        name="fox_attention",
    )(qkv, qkv, qkv)


def _gla_kernel(q_ref, k_ref, v_ref, sm_ref, gog_ref, wg_ref, bg_ref, ng_ref, o_ref,
                st_ref, *, rows, chunk):
    @pl.when(pl.program_id(2) == 0)
    def _():
        st_ref[...] = jnp.zeros_like(st_ref)

    z = _dot(sm_ref[...].astype(BF16), wg_ref[...]) + bg_ref[...]
    log_a = _log_sigmoid(z) * (1.0 / GLA_GATE_NORMALIZER)
    ri = lax.broadcasted_iota(jnp.int32, (chunk, chunk), 0)
    ci = lax.broadcasted_iota(jnp.int32, (chunk, chunk), 1)
    tril = ri >= ci
    ltri = tril.astype(F32)
    for c in range(rows // chunk):
        sl = slice(c * chunk, (c + 1) * chunk)
        bc = jnp.dot(ltri, log_a[sl], precision=lax.Precision.HIGHEST,
                     preferred_element_type=F32)
        b_last = bc[chunk - 1:chunk, :]
        kf = k_ref[sl, :].astype(F32)
        qe = (q_ref[sl, :].astype(F32) * (GLA_HK ** -0.5) * jnp.exp(bc)).astype(BF16)
        kd = (kf * jnp.exp(-bc)).astype(BF16)
        kdec = (kf * jnp.exp(b_last - bc)).astype(BF16)
        v = v_ref[sl, :]
        attn = jnp.where(tril, _dot_nt(qe, kd), 0.0)
        st = st_ref[...]
        o = _dot(attn.astype(BF16), v) + _dot_nt(qe, st.astype(BF16))
        st_ref[...] = st * jnp.exp(b_last) + _dot_tn(v, kdec)
        ms = jnp.mean(o * o, axis=-1, keepdims=True)
        gate = gog_ref[sl, :].astype(F32)
        y = o * lax.rsqrt(ms + RMS_EPS) * ng_ref[...] * (gate * jax.nn.sigmoid(gate))
        o_ref[sl, :] = y.astype(o_ref.dtype)


def _gla(rest, small, w_g2, b_g2, norm_g, batch, seq, rows, chunk):
    t = rest.shape[0]
    nr = seq // rows
    kq = 0
    kk = GLA_HEADS * GLA_HK // GLA_HK
    kv = 2 * GLA_HEADS * GLA_HK // GLA_HV
    kg = kv + GLA_HEADS
    return pl.pallas_call(
        functools.partial(_gla_kernel, rows=rows, chunk=chunk),
        grid=(batch, GLA_HEADS, nr),
        in_specs=[
            pl.BlockSpec((rows, GLA_HK), lambda b, h, s: (b * nr + s, kq + h)),
            pl.BlockSpec((rows, GLA_HK), lambda b, h, s: (b * nr + s, kk + h)),
            pl.BlockSpec((rows, GLA_HV), lambda b, h, s: (b * nr + s, kv + h)),
            pl.BlockSpec((rows, LANES), lambda b, h, s: (b * nr + s, 0)),
            pl.BlockSpec((rows, GLA_HV), lambda b, h, s: (b * nr + s, kg + h)),
            pl.BlockSpec((LANES, GLA_HK), lambda b, h, s: (0, h)),
            pl.BlockSpec((1, GLA_HK), lambda b, h, s: (0, h)),
            pl.BlockSpec((1, GLA_HV), lambda b, h, s: (0, 0)),
        ],
        out_specs=pl.BlockSpec((rows, GLA_HV), lambda b, h, s: (b * nr + s, h)),
        out_shape=jax.ShapeDtypeStruct((t, GLA_HEADS * GLA_HV), BF16),
        scratch_shapes=[pltpu.VMEM((GLA_HV, GLA_HK), F32)],
        compiler_params=_cparams("parallel", "parallel", "arbitrary"),
        name="gla",
    )(rest, rest, rest, small, rest, w_g2, b_g2, norm_g)


def _merge_kernel(x_ref, fox_ref, gla_ref, zf_ref, zg_ref, wf_ref, wg_ref, wo_ref,
                  g_ref, b_ref, o_ref, *, alpha):
    a = _dot(fox_ref[...], wf_ref[...])
    b = _dot(gla_ref[...], wg_ref[...])
    merged = (jax.nn.sigmoid(zf_ref[...].astype(F32)) * a
              + jax.nn.sigmoid(zg_ref[...].astype(F32)) * b)
    y = _dot(merged.astype(BF16), wo_ref[...])
    o_ref[...] = _layer_norm(alpha * x_ref[...] + y, g_ref[...], b_ref[...])


def _merge(x2, fox, gla, rest, wf, wg, wo, ln_g, ln_b, alpha, tm):
    t, d = x2.shape
    zf_blk = (rest.shape[1] - 2 * d) // d
    full = lambda i: (0, 0)
    return pl.pallas_call(
        functools.partial(_merge_kernel, alpha=alpha),
        grid=(t // tm,),
        in_specs=[
            pl.BlockSpec((tm, d), lambda i: (i, 0)),
            pl.BlockSpec((tm, d), lambda i: (i, 0)),
            pl.BlockSpec((tm, d), lambda i: (i, 0)),
            pl.BlockSpec((tm, d), lambda i: (i, zf_blk)),
            pl.BlockSpec((tm, d), lambda i: (i, zf_blk + 1)),
            pl.BlockSpec((d, d), full),
            pl.BlockSpec((d, d), full),
            pl.BlockSpec((d, d), full),
            pl.BlockSpec((1, d), full),
            pl.BlockSpec((1, d), full),
        ],
        out_specs=pl.BlockSpec((tm, d), lambda i: (i, 0)),
        out_shape=jax.ShapeDtypeStruct((t, d), F32),
        compiler_params=_cparams("parallel"),
        name="merge_ln",
    )(x2, fox, gla, rest, rest, wf, wg, wo, ln_g, ln_b)


def _top_rows(s, k):
    tm = s.shape[1]
    row = lax.broadcasted_iota(jnp.int32, (k, tm), 0)
    out = jnp.full((k, tm), -jnp.inf, F32)
    rank = jnp.full(s.shape, float(k), F32)
    cur = s
    for r in range(k):
        m = jnp.max(cur, axis=0, keepdims=True)
        out = jnp.where(row == r, m, out)
        hit = cur == m
        rank = jnp.where(hit, float(r), rank)
        cur = jnp.where(hit, -jnp.inf, cur)
    return out, rank


def _peerq_kernel(x_ref, wq_ref, sk_ref, rk_ref, bs_ref, lc_ref, ac_ref, *, tm):
    xb = x_ref[...].astype(BF16)
    qt = _dot_nt(wq_ref[...], xb)
    row8 = lax.broadcasted_iota(jnp.int32, (8, tm), 0)
    for h in range(PEER_HEADS):
        q1 = qt[(2 * h) * PEER_HALF:(2 * h + 1) * PEER_HALF].astype(BF16)
        q2 = qt[(2 * h + 1) * PEER_HALF:(2 * h + 2) * PEER_HALF].astype(BF16)
        s1 = _dot(sk_ref[2 * h], q1)
        s2 = _dot(sk_ref[2 * h + 1], q2)
        t1, rank1 = _top_rows(s1, PEER_TOPK)
        t2, rank2 = _top_rows(s2, PEER_TOPK)
        blocks = [t1[0:1] + t2]
        for a in range(1, 8):
            nb = PEER_TOPK // (a + 1)
            blocks.append(jnp.where(row8 < nb, t1[a:a + 1] + t2[0:8], -jnp.inf))
        blocks.append(t1[8:16] + t2[0:1])
        cand = jnp.concatenate(blocks, axis=0)
        best, _ = _top_rows(cand, PEER_TOPK)
        tau = best[PEER_TOPK - 1:PEER_TOPK]
        zsum = jnp.sum(jnp.exp(best - best[0:1]), axis=0, keepdims=True)
        count = jnp.zeros(s1.shape, F32)
        for a in range(PEER_TOPK):
            sel = (t1[a:a + 1] + t2 >= tau).astype(F32)
            count = jnp.where(rank1 == float(a), jnp.sum(sel, axis=0, keepdims=True), count)
        rk_ref[h] = rank2.astype(rk_ref.dtype)
        bs_ref[h] = jnp.exp(s2 - t2[0:1]).astype(bs_ref.dtype)
        lc_ref[h] = count
        ac_ref[h] = jnp.exp(s1 - t1[0:1]) / zsum


def _peer_query(x1, wq_t, subkeys, tm):
    t, d = x1.shape
    nq = wq_t.shape[0]
    shape = (PEER_HEADS, PEER_NKEYS, t)
    ospec = pl.BlockSpec((PEER_HEADS, PEER_NKEYS, tm), lambda i: (0, 0, i))
    return pl.pallas_call(
        functools.partial(_peerq_kernel, tm=tm),
        grid=(t // tm,),
        in_specs=[
            pl.BlockSpec((tm, d), lambda i: (i, 0)),
            pl.BlockSpec((nq, d), lambda i: (0, 0)),
            pl.BlockSpec((2 * PEER_HEADS, PEER_NKEYS, PEER_HALF), lambda i: (0, 0, 0)),
        ],
        out_specs=[ospec, ospec, ospec, ospec],
        out_shape=[jax.ShapeDtypeStruct(shape, BF16), jax.ShapeDtypeStruct(shape, BF16),
                   jax.ShapeDtypeStruct(shape, F32), jax.ShapeDtypeStruct(shape, F32)],
        compiler_params=_cparams("parallel"),
        name="peer_query",
    )(x1, wq_t, subkeys)


def _peer_dense_kernel(x_ref, u_ref, vt_ref, rk_ref, bs_ref, lcp_ref, lcc_ref, acp_ref, acc_ref_,
                       g_ref, b_ref, o_ref, xt_ref, acc_ref, ht0_ref, ht1_ref, act0_ref, act1_ref,
                       *, tb, tn, alpha):
    s = pl.program_id(1)
    nsub = 2 * (pl.num_programs(1) - 1)
    nblk = tb // PEER_NKEYS
    tm = x_ref.shape[0]
    ht_refs = (ht0_ref, ht1_ref)
    act_refs = (act0_ref, act1_ref)

    @pl.when(s == 0)
    def _():
        xt_ref[...] = x_ref[...].T.astype(BF16)
        acc_ref[...] = jnp.zeros_like(acc_ref)
        for ref in ht_refs + act_refs:
            ref[...] = jnp.zeros_like(ref)

    def stage_a(slot, c):
        cols = slice(c * tn, (c + 1) * tn)
        ht_refs[slot][:, cols] = _dot(u_ref[slot * tb:(slot + 1) * tb, :], xt_ref[:, cols])

    def stage_b(slot, k, j, c):
        live = jnp.logical_and(k >= 0, k < nsub).astype(F32)
        lc_ref, ac_ref, i1 = (lcp_ref, acp_ref, nblk + j) if slot == 1 else (lcc_ref, acc_ref_, j)
        rows = slice(j * PEER_NKEYS, (j + 1) * PEER_NKEYS)
        cols = slice(c * tn, (c + 1) * tn)
        hj = ht_refs[slot][rows, cols]
        g = 0.5 * hj * (1.0 + lax.erf(hj * np.float32(np.sqrt(0.5))))
        w = jnp.zeros(hj.shape, BF16)
        for h in range(PEER_HEADS):
            cnt = lc_ref[h, i1:i1 + 1, cols].astype(BF16)
            a = (ac_ref[h, i1:i1 + 1, cols] * live).astype(BF16)
            w = w + jnp.where(rk_ref[h, :, cols] < cnt, bs_ref[h, :, cols],
                              jnp.zeros((), BF16)) * a
        act_refs[slot][rows, cols] = g.astype(BF16) * w

    def stage_c(slot, c):
        cols = slice(c * tn, (c + 1) * tn)
        acc_ref[:, cols] += _dot(vt_ref[:, slot * tb:(slot + 1) * tb], act_refs[slot][:, cols])

    nc = tm // tn
    for slot_b, k in ((1, 2 * s - 1), (0, 2 * s)):
        slot_m = 1 - slot_b
        gate = [(j, c) for c in range(nc) for j in range(nblk)]
        per = -(-len(gate) // (2 * nc))
        mxu = [(stage_c, slot_m, c) for c in range(nc)] + [(stage_a, slot_m, c) for c in range(nc)]
        for n, (fn, slot, c) in enumerate(mxu):
            fn(slot, c)
            for j, cb in gate[n * per:(n + 1) * per]:
                stage_b(slot_b, k, j, cb)

    @pl.when(s == pl.num_programs(1) - 1)
    def _():
        r = alpha * x_ref[...] + acc_ref[...].T
        o_ref[...] = _layer_norm(r, g_ref[...], b_ref[...])


def _peer_dense(x1, u, vt, rk, bs, lc, ac, ln_g, ln_b, alpha, tm, tb, tn):
    t, d = x1.shape
    ns = u.shape[0] // (2 * tb)
    sel = pl.BlockSpec((PEER_HEADS, PEER_NKEYS, tm), lambda i, s: (0, 0, i))
    nrow = 2 * tb // PEER_NKEYS
    row_prev = pl.BlockSpec((PEER_HEADS, nrow, tm), lambda i, s: (0, jnp.maximum(s - 1, 0), i))
    row_cur = pl.BlockSpec((PEER_HEADS, nrow, tm), lambda i, s: (0, jnp.minimum(s, ns - 1), i))
    return pl.pallas_call(
        functools.partial(_peer_dense_kernel, tb=tb, tn=tn, alpha=alpha),
        grid=(t // tm, ns + 1),
        in_specs=[
            pl.BlockSpec((tm, d), lambda i, s: (i, 0)),
            pl.BlockSpec((2 * tb, d), lambda i, s: (jnp.minimum(s, ns - 1), 0)),
            pl.BlockSpec((d, 2 * tb), lambda i, s: (0, jnp.maximum(s - 1, 0))),
            sel, sel, row_prev, row_cur, row_prev, row_cur,
            pl.BlockSpec((1, d), lambda i, s: (0, 0)),
            pl.BlockSpec((1, d), lambda i, s: (0, 0)),
        ],
        out_specs=pl.BlockSpec((tm, d), lambda i, s: (i, 0)),
        out_shape=jax.ShapeDtypeStruct((t, d), F32),
        scratch_shapes=[
            pltpu.VMEM((d, tm), BF16),
            pltpu.VMEM((d, tm), F32),
            pltpu.VMEM((tb, tm), F32),
            pltpu.VMEM((tb, tm), F32),
            pltpu.VMEM((tb, tm), BF16),
            pltpu.VMEM((tb, tm), BF16),
        ],
        compiler_params=_cparams("parallel", "arbitrary"),
        name="peer_dense",
    )(x1, u, vt, rk, bs, lc, lc, ac, ac, ln_g, ln_b)


def _aug_weight():
    wa = np.zeros((LANES, 3 * FOX_HEADS * LANES), np.float32)
    kb = FOX_HEADS * LANES
    for h in range(FOX_HEADS):
        qc, kc, vc = h * LANES, kb + h * LANES, 2 * kb + h * LANES
        wa[AUG_HI + h, qc + 64] = 1.0
        wa[AUG_MID + h, qc + 65] = 1.0
        wa[AUG_LO + h, qc + 66] = 1.0
        wa[AUG_ONE, qc + 67:qc + 70] = 1.0
        wa[AUG_ONE, kc + 64:kc + 67] = 1.0
        wa[AUG_HI + h, kc + 67] = -1.0
        wa[AUG_MID + h, kc + 68] = -1.0
        wa[AUG_LO + h, kc + 69] = -1.0
        wa[AUG_ONE, vc + (64 if h % 2 == 0 else 0)] = 1.0
    return wa


def _split_w_in(w):
    d = D_MODEL
    fw = FOX_HEADS * FOX_HEAD_DIM
    gk = GLA_HEADS * GLA_HK
    gv = GLA_HEADS * GLA_HV
    sizes = (fw, fw, fw, FOX_HEADS, gk, gk, gv, GLA_GATE_RANK, gv, d, d)
    pts = np.cumsum(sizes)[:-1]
    return jnp.split(w, [int(p) for p in pts], axis=1)


def _fox_weight(fq, fk, fv):
    d = fq.shape[0]
    pad = FOX_HEAD_DIM

    def heads(w, scale):
        w = w.reshape(d, FOX_HEADS, FOX_HEAD_DIM) * scale
        return jnp.pad(w, ((0, 0), (0, 0), (0, pad))).reshape(d, FOX_HEADS * LANES)

    v = fv.reshape(d, FOX_HEADS // 2, 2, FOX_HEAD_DIM)
    v_even = jnp.pad(v[:, :, 0], ((0, 0), (0, 0), (0, pad)))
    v_odd = jnp.pad(v[:, :, 1], ((0, 0), (0, 0), (pad, 0)))
    wv = jnp.stack([v_even, v_odd], axis=2).reshape(d, FOX_HEADS * LANES)
    return jnp.concatenate([heads(fq, FOX_HEAD_DIM ** -0.5), heads(fk, 1.0), wv], axis=1)


def _pick(n, pref):
    return pref if n % pref == 0 else n


def kernel(x, w_in, b_fgate, w_gla_g2, b_gla_g2, gla_norm_g, w_br_fox, w_br_gla, w_o,
           ln1_g, ln1_b, peer_wq, peer_subkeys, peer_u, peer_v, ln2_g, ln2_b):
    batch, seq, d = x.shape
    depth = w_in.shape[0]
    t = batch * seq
    alpha = float((2.0 * depth) ** 0.25)
    w_aug = jnp.asarray(_aug_weight(), BF16)

    x2 = x.reshape(t, d)
    for l in range(depth):
        fq, fk, fv, ff, gq, gk, gv, gg1, gog, zf, zg = _split_w_in(w_in[l])
        w_fox = _fox_weight(fq, fk, fv).astype(BF16)
        w_rest = jnp.concatenate([gq, gk, gv, gog, zf, zg], axis=1).astype(BF16)
        w_small = jnp.pad(jnp.concatenate([ff, gg1], axis=1),
                          ((0, 0), (0, LANES - FOX_HEADS - GLA_GATE_RANK))).astype(BF16)
        bias_f = jnp.pad(b_fgate[l], (0, LANES - FOX_HEADS)).reshape(1, LANES)
        w_g2 = jnp.pad(w_gla_g2[l], ((FOX_HEADS, LANES - FOX_HEADS - GLA_GATE_RANK),
                                     (0, 0))).astype(BF16)
        b_g2 = b_gla_g2[l].reshape(1, -1)
        norm_g = gla_norm_g[l].reshape(1, -1)

        tm_proj = _pick(t, 1024)
        rest, small = _proj_rest(x2, w_rest, w_small, tm_proj, 512)
        xaug = _forget_cumsum(small, bias_f, batch, seq, _pick(seq, 512))
        qkv = _proj_fox(x2, xaug, w_fox, w_aug, tm_proj, 512)
        fox = _fox_attention(qkv, batch, seq, _pick(seq, 512))
        gla = _gla(rest, small, w_g2, b_g2, norm_g, batch, seq, _pick(seq, 256), 64)
        x1 = _merge(x2, fox, gla, rest, w_br_fox[l].astype(BF16), w_br_gla[l].astype(BF16),
                    w_o[l].astype(BF16), ln1_g[l].reshape(1, d), ln1_b[l].reshape(1, d),
                    alpha, _pick(t, 512))

        wq_t = peer_wq[l].T.astype(BF16)
        subkeys = peer_subkeys[l].reshape(2 * PEER_HEADS, PEER_NKEYS, PEER_HALF).astype(BF16)
        rk, bs, lc, ac = _peer_query(x1, wq_t, subkeys, _pick(t, 256))
        x2 = _peer_dense(x1, peer_u[l].astype(BF16), peer_v[l].T.astype(BF16), rk, bs, lc, ac,
                         ln2_g[l].reshape(1, d), ln2_b[l].reshape(1, d), alpha,
                         _pick(t, 512), 512, 256)
    return x2.reshape(batch, seq, d)
```

```python
import functools

import numpy as np
import jax
import jax.numpy as jnp
from jax import lax
from jax.experimental import pallas as pl
from jax.experimental.pallas import tpu as pltpu

F32 = jnp.float32
BF16 = jnp.bfloat16

D_MODEL = 1024
FOX_HEADS = 16
FOX_HEAD_DIM = 64
GLA_HEADS = 4
GLA_HK = 128
GLA_HV = 256
GLA_GATE_RANK = 16
GLA_GATE_NORMALIZER = 16.0
PEER_HEADS = 8
PEER_NKEYS = 128
PEER_HALF = 128
PEER_TOPK = 16
LN_EPS = 1e-5
RMS_EPS = 1e-6

LANES = 128
VMEM_LIMIT = 56 * 1024 * 1024

AUG_HI, AUG_MID, AUG_LO, AUG_ONE = 0, 16, 32, 48
NEG_BIG = -1e30
LOG2E = 1.4426950408889634


def _cparams(*sem):
    return pltpu.CompilerParams(dimension_semantics=sem, vmem_limit_bytes=VMEM_LIMIT)


def _log_sigmoid(z):
    return jnp.minimum(z, 0.0) - jnp.log1p(jnp.exp(-jnp.abs(z)))


def _layer_norm(r, g, b):
    mu = jnp.mean(r, axis=-1, keepdims=True)
    d = r - mu
    var = jnp.mean(d * d, axis=-1, keepdims=True)
    return d * lax.rsqrt(var + LN_EPS) * g + b


def _dot(a, b):
    return jnp.dot(a, b, preferred_element_type=F32)


def _dot_nt(a, b):
    return lax.dot_general(a, b, (((1,), (1,)), ((), ())), preferred_element_type=F32)


def _dot_tn(a, b):
    return lax.dot_general(a, b, (((0,), (0,)), ((), ())), preferred_element_type=F32)


def _proj_rest_kernel(x_ref, w_ref, ws_ref, o_ref, os_ref):
    xb = x_ref[...].astype(BF16)
    o_ref[...] = _dot(xb, w_ref[...]).astype(o_ref.dtype)

    @pl.when(pl.program_id(1) == 0)
    def _():
        os_ref[...] = _dot(xb, ws_ref[...])


def _proj_fox_kernel(x_ref, xa_ref, w_ref, wa_ref, o_ref):
    acc = _dot(x_ref[...].astype(BF16), w_ref[...]) + _dot(xa_ref[...], wa_ref[...])
    o_ref[...] = acc.astype(o_ref.dtype)


def _proj_rest(x2, w_rest, w_small, tm, tn):
    t, d = x2.shape
    n = w_rest.shape[1]
    return pl.pallas_call(
        _proj_rest_kernel,
        grid=(t // tm, n // tn),
        in_specs=[
            pl.BlockSpec((tm, d), lambda i, j: (i, 0)),
            pl.BlockSpec((d, tn), lambda i, j: (0, j)),
            pl.BlockSpec((d, LANES), lambda i, j: (0, 0)),
        ],
        out_specs=[
            pl.BlockSpec((tm, tn), lambda i, j: (i, j)),
            pl.BlockSpec((tm, LANES), lambda i, j: (i, 0)),
        ],
        out_shape=[
            jax.ShapeDtypeStruct((t, n), BF16),
            jax.ShapeDtypeStruct((t, LANES), F32),
        ],
        compiler_params=_cparams("parallel", "arbitrary"),
        name="proj_rest",
    )(x2, w_rest, w_small)


def _proj_fox(x2, xaug, w_fox, w_aug, tm, tn):
    t, d = x2.shape
    n = w_fox.shape[1]
    return pl.pallas_call(
        _proj_fox_kernel,
        grid=(t // tm, n // tn),
        in_specs=[
            pl.BlockSpec((tm, d), lambda i, j: (i, 0)),
            pl.BlockSpec((tm, LANES), lambda i, j: (i, 0)),
            pl.BlockSpec((d, tn), lambda i, j: (0, j)),
            pl.BlockSpec((LANES, tn), lambda i, j: (0, j)),
        ],
        out_specs=pl.BlockSpec((tm, tn), lambda i, j: (i, j)),
        out_shape=jax.ShapeDtypeStruct((t, n), BF16),
        compiler_params=_cparams("parallel", "arbitrary"),
        name="proj_fox",
    )(x2, xaug, w_fox, w_aug)


def _cumsum_kernel(s_ref, bias_ref, o_ref, carry_ref, *, tc):
    @pl.when(pl.program_id(1) == 0)
    def _():
        carry_ref[...] = jnp.zeros_like(carry_ref)

    z = s_ref[...] + bias_ref[...]
    lane = lax.broadcasted_iota(jnp.int32, z.shape, 1)
    logf = jnp.where(lane < FOX_HEADS, _log_sigmoid(z), 0.0)
    row = lax.broadcasted_iota(jnp.int32, (tc, tc), 0)
    col = lax.broadcasted_iota(jnp.int32, (tc, tc), 1)
    ltri = (row >= col).astype(F32)
    c = jnp.dot(ltri, logf, precision=lax.Precision.HIGHEST,
                preferred_element_type=F32) + carry_ref[0:1, :]
    carry_ref[...] = jnp.broadcast_to(c[tc - 1:tc, :], carry_ref.shape)
    c = c * LOG2E
    c_mid = pltpu.roll(c, AUG_MID, 1)
    c_lo = pltpu.roll(c, AUG_LO, 1)
    csel = jnp.where(lane < AUG_MID, c, jnp.where(lane < AUG_LO, c_mid, c_lo))
    hi = csel.astype(BF16).astype(F32)
    r1 = csel - hi
    mid = r1.astype(BF16).astype(F32)
    lo = (r1 - mid).astype(BF16).astype(F32)
    val = jnp.where(lane < AUG_MID, hi,
                    jnp.where(lane < AUG_LO, mid,
                              jnp.where(lane < AUG_ONE, lo,
                                        jnp.where(lane == AUG_ONE, 1.0, 0.0))))
    o_ref[...] = val.astype(o_ref.dtype)


def _forget_cumsum(small, bias, batch, seq, tc):
    t = small.shape[0]
    nb = seq // tc
    return pl.pallas_call(
        functools.partial(_cumsum_kernel, tc=tc),
        grid=(batch, nb),
        in_specs=[
            pl.BlockSpec((tc, LANES), lambda b, s: (b * nb + s, 0)),
            pl.BlockSpec((1, LANES), lambda b, s: (0, 0)),
        ],
        out_specs=pl.BlockSpec((tc, LANES), lambda b, s: (b * nb + s, 0)),
        out_shape=jax.ShapeDtypeStruct((t, LANES), BF16),
        scratch_shapes=[pltpu.VMEM((8, LANES), F32)],
        compiler_params=_cparams("parallel", "arbitrary"),
        name="forget_cumsum",
    )(small, bias)


def _fox_kernel(q_ref, k_ref, v_ref, o_ref, m_ref, acc_ref, *, tq):
    qi = pl.program_id(2)
    lane = lax.broadcasted_iota(jnp.int32, (tq, LANES), 1)
    row = lax.broadcasted_iota(jnp.int32, (tq, tq), 0)
    col = lax.broadcasted_iota(jnp.int32, (tq, tq), 1)
    causal = row >= col
    m_ref[...] = jnp.full(m_ref.shape, NEG_BIG, F32)
    acc_ref[...] = jnp.zeros(acc_ref.shape, F32)

    def step(kb, masked):
        start = pl.multiple_of(kb * tq, tq)
        for hh in range(2):
            lanes = slice(hh * LANES, (hh + 1) * LANES)
            k = k_ref[pl.ds(start, tq), lanes]
            v = v_ref[pl.ds(start, tq), lanes]
            s = _dot_nt(q_ref[:, lanes], k)
            if masked:
                s = jnp.where(causal, s, NEG_BIG)
            m_prev = m_ref[hh]
            m_next = jnp.maximum(m_prev, jnp.max(s, axis=1, keepdims=True))
            p = jnp.exp2(s - jnp.tile(m_next, (1, tq // LANES)))
            alpha = jnp.exp2(m_prev - m_next)
            acc_ref[hh] = acc_ref[hh] * alpha + _dot(p.astype(BF16), v)
            m_ref[hh] = m_next

    def body(kb, carry):
        step(kb, False)
        return carry

    lax.fori_loop(0, qi, body, 0)
    step(qi, True)
    outs = []
    for hh in range(2):
        acc = acc_ref[hh]
        l_lane = 64 if hh == 0 else 0
        l = jnp.sum(jnp.where(lane == l_lane, acc, 0.0), axis=1, keepdims=True)
        outs.append(acc / l)
    o_ref[...] = jnp.where(lane < 64, outs[0], outs[1]).astype(o_ref.dtype)


def _fox_attention(qkv, batch, seq, tq):
    t = qkv.shape[0]
    nq = seq // tq
    npair = FOX_HEADS // 2
    return pl.pallas_call(
        functools.partial(_fox_kernel, tq=tq),
        grid=(batch, npair, nq),
        in_specs=[
            pl.BlockSpec((tq, 2 * LANES), lambda b, h, i: (b * nq + i, h)),
            pl.BlockSpec((seq, 2 * LANES), lambda b, h, i: (b, npair + h)),
            pl.BlockSpec((seq, 2 * LANES), lambda b, h, i: (b, 2 * npair + h)),
        ],
        out_specs=pl.BlockSpec((tq, LANES), lambda b, h, i: (b * nq + i, h)),
        out_shape=jax.ShapeDtypeStruct((t, FOX_HEADS * FOX_HEAD_DIM), BF16),
        scratch_shapes=[pltpu.VMEM((2, tq, LANES), F32), pltpu.VMEM((2, tq, LANES), F32)],
        compiler_params=_cparams("parallel", "parallel", "arbitrary"),
        name="fox_attention",
    )(qkv, qkv, qkv)


def _gla_kernel(q_ref, k_ref, v_ref, sm_ref, gog_ref, wg_ref, bg_ref, ng_ref, o_ref,
                st_ref, *, rows, chunk):
    @pl.when(pl.program_id(1) == 0)
    def _():
        st_ref[...] = jnp.zeros_like(st_ref)

    z = _dot(sm_ref[...].astype(BF16), wg_ref[...]) + bg_ref[...]
    log_a = _log_sigmoid(z) * (1.0 / GLA_GATE_NORMALIZER)
    ri = lax.broadcasted_iota(jnp.int32, (chunk, chunk), 0)
    ci = lax.broadcasted_iota(jnp.int32, (chunk, chunk), 1)
    tril = ri >= ci
    ltri = tril.astype(F32)
    for c in range(rows // chunk):
        sl = slice(c * chunk, (c + 1) * chunk)
        bc_all = jnp.dot(ltri, log_a[sl], precision=lax.Precision.HIGHEST,
                         preferred_element_type=F32)
        for h in range(GLA_HEADS):
            kl = slice(h * GLA_HK, (h + 1) * GLA_HK)
            vl = slice(h * GLA_HV, (h + 1) * GLA_HV)
            bc = bc_all[:, kl]
            b_last = bc[chunk - 1:chunk, :]
            kf = k_ref[sl, kl].astype(F32)
            qe = (q_ref[sl, kl].astype(F32) * (GLA_HK ** -0.5) * jnp.exp(bc)).astype(BF16)
            kd = (kf * jnp.exp(-bc)).astype(BF16)
            kdec = (kf * jnp.exp(b_last - bc)).astype(BF16)
            v = v_ref[sl, vl]
            attn = jnp.where(tril, _dot_nt(qe, kd), 0.0)
            st = st_ref[h]
            o = _dot(attn.astype(BF16), v) + _dot_nt(qe, st.astype(BF16))
            st_ref[h] = st * jnp.exp(b_last) + _dot_tn(v, kdec)
            ms = jnp.mean(o * o, axis=-1, keepdims=True)
            gate = gog_ref[sl, vl].astype(F32)
            y = o * lax.rsqrt(ms + RMS_EPS) * ng_ref[...] * (gate * jax.nn.sigmoid(gate))
            o_ref[sl, vl] = y.astype(o_ref.dtype)


def _gla(rest, small, w_g2, b_g2, norm_g, batch, seq, rows, chunk):
    t = rest.shape[0]
    nr = seq // rows
    wk = GLA_HEADS * GLA_HK
    wv = GLA_HEADS * GLA_HV
    row = lambda b, s: b * nr + s
    return pl.pallas_call(
        functools.partial(_gla_kernel, rows=rows, chunk=chunk),
        grid=(batch, nr),
        in_specs=[
            pl.BlockSpec((rows, wk), lambda b, s: (row(b, s), 0)),
            pl.BlockSpec((rows, wk), lambda b, s: (row(b, s), 1)),
            pl.BlockSpec((rows, wv), lambda b, s: (row(b, s), 2 * wk // wv)),
            pl.BlockSpec((rows, LANES), lambda b, s: (row(b, s), 0)),
            pl.BlockSpec((rows, wv), lambda b, s: (row(b, s), 2 * wk // wv + 1)),
            pl.BlockSpec((LANES, wk), lambda b, s: (0, 0)),
            pl.BlockSpec((1, wk), lambda b, s: (0, 0)),
            pl.BlockSpec((1, GLA_HV), lambda b, s: (0, 0)),
        ],
        out_specs=pl.BlockSpec((rows, wv), lambda b, s: (row(b, s), 0)),
        out_shape=jax.ShapeDtypeStruct((t, wv), BF16),
        scratch_shapes=[pltpu.VMEM((GLA_HEADS, GLA_HV, GLA_HK), F32)],
        compiler_params=_cparams("parallel", "arbitrary"),
        name="gla",
    )(rest, rest, rest, small, rest, w_g2, b_g2, norm_g)


def _merge_kernel(x_ref, fox_ref, gla_ref, zf_ref, zg_ref, wf_ref, wg_ref, wo_ref,
                  g_ref, b_ref, o_ref, *, alpha):
    a = _dot(fox_ref[...], wf_ref[...])
    b = _dot(gla_ref[...], wg_ref[...])
    merged = (jax.nn.sigmoid(zf_ref[...].astype(F32)) * a
              + jax.nn.sigmoid(zg_ref[...].astype(F32)) * b)
    y = _dot(merged.astype(BF16), wo_ref[...])
    o_ref[...] = _layer_norm(alpha * x_ref[...] + y, g_ref[...], b_ref[...])


def _merge(x2, fox, gla, rest, wf, wg, wo, ln_g, ln_b, alpha, tm):
    t, d = x2.shape
    zf_blk = (rest.shape[1] - 2 * d) // d
    full = lambda i: (0, 0)
    return pl.pallas_call(
        functools.partial(_merge_kernel, alpha=alpha),
        grid=(t // tm,),
        in_specs=[
            pl.BlockSpec((tm, d), lambda i: (i, 0)),
            pl.BlockSpec((tm, d), lambda i: (i, 0)),
            pl.BlockSpec((tm, d), lambda i: (i, 0)),
            pl.BlockSpec((tm, d), lambda i: (i, zf_blk)),
            pl.BlockSpec((tm, d), lambda i: (i, zf_blk + 1)),
            pl.BlockSpec((d, d), full),
            pl.BlockSpec((d, d), full),
            pl.BlockSpec((d, d), full),
            pl.BlockSpec((1, d), full),
            pl.BlockSpec((1, d), full),
        ],
        out_specs=pl.BlockSpec((tm, d), lambda i: (i, 0)),
        out_shape=jax.ShapeDtypeStruct((t, d), F32),
        compiler_params=_cparams("parallel"),
        name="merge_ln",
    )(x2, fox, gla, rest, rest, wf, wg, wo, ln_g, ln_b)


def _top_rows(s, k, with_rank=False):
    tm = s.shape[1]
    row = lax.broadcasted_iota(jnp.int32, (k, tm), 0)
    out = jnp.full((k, tm), -jnp.inf, F32)
    rank = jnp.full(s.shape, float(k), F32)
    cur = s
    for r in range(k):
        m = jnp.max(cur, axis=0, keepdims=True)
        out = jnp.where(row == r, m, out)
        hit = cur == m
        if with_rank:
            rank = jnp.where(hit, float(r), rank)
        cur = jnp.where(hit, -jnp.inf, cur)
    return (out, rank) if with_rank else out


def _peerq_kernel(x_ref, wq_ref, sk_ref, rk_ref, bs_ref, lc_ref, ac_ref, *, tm):
    xb = x_ref[...].astype(BF16)
    qt = _dot_nt(wq_ref[...], xb)
    row8 = lax.broadcasted_iota(jnp.int32, (8, tm), 0)
    for h in range(PEER_HEADS):
        q1 = qt[(2 * h) * PEER_HALF:(2 * h + 1) * PEER_HALF].astype(BF16)
        q2 = qt[(2 * h + 1) * PEER_HALF:(2 * h + 2) * PEER_HALF].astype(BF16)
        s1 = _dot(sk_ref[2 * h], q1)
        s2 = _dot(sk_ref[2 * h + 1], q2)
        t1 = _top_rows(s1, PEER_TOPK)
        t2, rank2 = _top_rows(s2, PEER_TOPK, with_rank=True)
        blocks = [t1[0:1] + t2]
        for a in range(1, 8):
            nb = PEER_TOPK // (a + 1)
            blocks.append(jnp.where(row8 < nb, t1[a:a + 1] + t2[0:8], -jnp.inf))
        blocks.append(t1[8:16] + t2[0:1])
        cand = jnp.concatenate(blocks, axis=0)
        best = _top_rows(cand, PEER_TOPK)
        tau = best[PEER_TOPK - 1:PEER_TOPK]
        zsum = jnp.sum(jnp.exp(best - best[0:1]), axis=0, keepdims=True)
        count = jnp.zeros(s1.shape, F32)
        for a in range(PEER_TOPK):
            t1a = t1[a:a + 1]
            sel = (t1a + t2 >= tau).astype(F32)
            count = jnp.where(s1 == t1a, jnp.sum(sel, axis=0, keepdims=True), count)
        rk_ref[h] = rank2.astype(rk_ref.dtype)
        bs_ref[h] = jnp.exp(s2 - t2[0:1]).astype(bs_ref.dtype)
        lc_ref[h] = count
        ac_ref[h] = jnp.exp(s1 - t1[0:1]) / zsum


def _peer_query(x1, wq_t, subkeys, tm):
    t, d = x1.shape
    nq = wq_t.shape[0]
    shape = (PEER_HEADS, PEER_NKEYS, t)
    ospec = pl.BlockSpec((PEER_HEADS, PEER_NKEYS, tm), lambda i: (0, 0, i))
    return pl.pallas_call(
        functools.partial(_peerq_kernel, tm=tm),
        grid=(t // tm,),
        in_specs=[
            pl.BlockSpec((tm, d), lambda i: (i, 0)),
            pl.BlockSpec((nq, d), lambda i: (0, 0)),
            pl.BlockSpec((2 * PEER_HEADS, PEER_NKEYS, PEER_HALF), lambda i: (0, 0, 0)),
        ],
        out_specs=[ospec, ospec, ospec, ospec],
        out_shape=[jax.ShapeDtypeStruct(shape, BF16), jax.ShapeDtypeStruct(shape, BF16),
                   jax.ShapeDtypeStruct(shape, F32), jax.ShapeDtypeStruct(shape, F32)],
        compiler_params=_cparams("parallel"),
        name="peer_query",
    )(x1, wq_t, subkeys)


def _peer_dense_kernel(x_ref, u_ref, vt_ref, rk_ref, bs_ref, lcp_ref, lcc_ref, acp_ref, acc_ref_,
                       g_ref, b_ref, o_ref, xt_ref, acc_ref, ht0_ref, ht1_ref, act0_ref, act1_ref,
                       *, tb, tn, alpha):
    s = pl.program_id(1)
    nsub = 2 * (pl.num_programs(1) - 1)
    nblk = tb // PEER_NKEYS
    tm = x_ref.shape[0]
    ht_refs = (ht0_ref, ht1_ref)
    act_refs = (act0_ref, act1_ref)

    @pl.when(s == 0)
    def _():
        xt_ref[...] = x_ref[...].T.astype(BF16)
        acc_ref[...] = jnp.zeros_like(acc_ref)
        for ref in ht_refs + act_refs:
            ref[...] = jnp.zeros_like(ref)

    def stage_a(slot, c):
        cols = slice(c * tn, (c + 1) * tn)
        ht_refs[slot][:, cols] = _dot(u_ref[slot * tb:(slot + 1) * tb, :], xt_ref[:, cols])

    def stage_b(slot, k, j, c):
        live = jnp.logical_and(k >= 0, k < nsub).astype(F32)
        lc_ref, ac_ref, i1 = (lcp_ref, acp_ref, nblk + j) if slot == 1 else (lcc_ref, acc_ref_, j)
        rows = slice(j * PEER_NKEYS, (j + 1) * PEER_NKEYS)
        cols = slice(c * tn, (c + 1) * tn)
        hj = ht_refs[slot][rows, cols].astype(BF16)
        g = (0.5 * hj) * (1.0 + lax.erf(hj * np.sqrt(0.5).astype(BF16)))
        w = jnp.zeros(hj.shape, BF16)
        for h in range(PEER_HEADS):
            cnt = lc_ref[h, i1:i1 + 1, cols].astype(BF16)
            a = (ac_ref[h, i1:i1 + 1, cols] * live).astype(BF16)
            w = w + jnp.where(rk_ref[h, :, cols] < cnt, bs_ref[h, :, cols],
                              jnp.zeros((), BF16)) * a
        act_refs[slot][rows, cols] = g * w

    def stage_c(slot, c):
        cols = slice(c * tn, (c + 1) * tn)
        acc_ref[:, cols] += _dot(vt_ref[:, slot * tb:(slot + 1) * tb], act_refs[slot][:, cols])

    nc = tm // tn
    for slot_b, k in ((1, 2 * s - 1), (0, 2 * s)):
        slot_m = 1 - slot_b
        gate = [(j, c) for c in range(nc) for j in range(nblk)]
        per = -(-len(gate) // (2 * nc))
        mxu = [(stage_c, slot_m, c) for c in range(nc)] + [(stage_a, slot_m, c) for c in range(nc)]
        for n, (fn, slot, c) in enumerate(mxu):
            fn(slot, c)
            for j, cb in gate[n * per:(n + 1) * per]:
                stage_b(slot_b, k, j, cb)

    @pl.when(s == pl.num_programs(1) - 1)
    def _():
        r = alpha * x_ref[...] + acc_ref[...].T
        o_ref[...] = _layer_norm(r, g_ref[...], b_ref[...])


def _peer_dense(x1, u, vt, rk, bs, lc, ac, ln_g, ln_b, alpha, tm, tb, tn):
    t, d = x1.shape
    ns = u.shape[0] // (2 * tb)
    sel = pl.BlockSpec((PEER_HEADS, PEER_NKEYS, tm), lambda i, s: (0, 0, i))
    nrow = 2 * tb // PEER_NKEYS
    row_prev = pl.BlockSpec((PEER_HEADS, nrow, tm), lambda i, s: (0, jnp.maximum(s - 1, 0), i))
    row_cur = pl.BlockSpec((PEER_HEADS, nrow, tm), lambda i, s: (0, jnp.minimum(s, ns - 1), i))
    return pl.pallas_call(
        functools.partial(_peer_dense_kernel, tb=tb, tn=tn, alpha=alpha),
        grid=(t // tm, ns + 1),
        in_specs=[
            pl.BlockSpec((tm, d), lambda i, s: (i, 0)),
            pl.BlockSpec((2 * tb, d), lambda i, s: (jnp.minimum(s, ns - 1), 0)),
            pl.BlockSpec((d, 2 * tb), lambda i, s: (0, jnp.maximum(s - 1, 0))),
            sel, sel, row_prev, row_cur, row_prev, row_cur,
            pl.BlockSpec((1, d), lambda i, s: (0, 0)),
            pl.BlockSpec((1, d), lambda i, s: (0, 0)),
        ],
        out_specs=pl.BlockSpec((tm, d), lambda i, s: (i, 0)),
        out_shape=jax.ShapeDtypeStruct((t, d), F32),
        scratch_shapes=[
            pltpu.VMEM((d, tm), BF16),
            pltpu.VMEM((d, tm), F32),
            pltpu.VMEM((tb, tm), F32),
            pltpu.VMEM((tb, tm), F32),
            pltpu.VMEM((tb, tm), BF16),
            pltpu.VMEM((tb, tm), BF16),
        ],
        compiler_params=_cparams("parallel", "arbitrary"),
        name="peer_dense",
    )(x1, u, vt, rk, bs, lc, lc, ac, ac, ln_g, ln_b)


def _aug_weight():
    wa = np.zeros((LANES, 3 * FOX_HEADS * LANES), np.float32)
    kb = FOX_HEADS * LANES
    for h in range(FOX_HEADS):
        qc, kc, vc = h * LANES, kb + h * LANES, 2 * kb + h * LANES
        wa[AUG_HI + h, qc + 64] = 1.0
        wa[AUG_MID + h, qc + 65] = 1.0
        wa[AUG_LO + h, qc + 66] = 1.0
        wa[AUG_ONE, qc + 67:qc + 70] = 1.0
        wa[AUG_ONE, kc + 64:kc + 67] = 1.0
        wa[AUG_HI + h, kc + 67] = -1.0
        wa[AUG_MID + h, kc + 68] = -1.0
        wa[AUG_LO + h, kc + 69] = -1.0
        wa[AUG_ONE, vc + (64 if h % 2 == 0 else 0)] = 1.0
    return wa


def _split_w_in(w):
    d = D_MODEL
    fw = FOX_HEADS * FOX_HEAD_DIM
    gk = GLA_HEADS * GLA_HK
    gv = GLA_HEADS * GLA_HV
    sizes = (fw, fw, fw, FOX_HEADS, gk, gk, gv, GLA_GATE_RANK, gv, d, d)
    pts = np.cumsum(sizes)[:-1]
    return jnp.split(w, [int(p) for p in pts], axis=1)


def _fox_weight(fq, fk, fv):
    d = fq.shape[0]
    pad = FOX_HEAD_DIM

    def heads(w, scale):
        w = w.reshape(d, FOX_HEADS, FOX_HEAD_DIM) * scale
        return jnp.pad(w, ((0, 0), (0, 0), (0, pad))).reshape(d, FOX_HEADS * LANES)

    v = fv.reshape(d, FOX_HEADS // 2, 2, FOX_HEAD_DIM)
    v_even = jnp.pad(v[:, :, 0], ((0, 0), (0, 0), (0, pad)))
    v_odd = jnp.pad(v[:, :, 1], ((0, 0), (0, 0), (pad, 0)))
    wv = jnp.stack([v_even, v_odd], axis=2).reshape(d, FOX_HEADS * LANES)
    return jnp.concatenate([heads(fq, FOX_HEAD_DIM ** -0.5 * LOG2E), heads(fk, 1.0), wv], axis=1)


def _pick(n, pref):
    return pref if n % pref == 0 else n


def kernel(x, w_in, b_fgate, w_gla_g2, b_gla_g2, gla_norm_g, w_br_fox, w_br_gla, w_o,
           ln1_g, ln1_b, peer_wq, peer_subkeys, peer_u, peer_v, ln2_g, ln2_b):
    batch, seq, d = x.shape
    depth = w_in.shape[0]
    t = batch * seq
    alpha = float((2.0 * depth) ** 0.25)
    w_aug = jnp.asarray(_aug_weight(), BF16)

    x2 = x.reshape(t, d)
    for l in range(depth):
        fq, fk, fv, ff, gq, gk, gv, gg1, gog, zf, zg = _split_w_in(w_in[l])
        w_fox = _fox_weight(fq, fk, fv).astype(BF16)
        w_rest = jnp.concatenate([gq, gk, gv, gog, zf, zg], axis=1).astype(BF16)
        w_small = jnp.pad(jnp.concatenate([ff, gg1], axis=1),
                          ((0, 0), (0, LANES - FOX_HEADS - GLA_GATE_RANK))).astype(BF16)
        bias_f = jnp.pad(b_fgate[l], (0, LANES - FOX_HEADS)).reshape(1, LANES)
        w_g2 = jnp.pad(w_gla_g2[l], ((FOX_HEADS, LANES - FOX_HEADS - GLA_GATE_RANK),
                                     (0, 0))).astype(BF16)
        b_g2 = b_gla_g2[l].reshape(1, -1)
        norm_g = gla_norm_g[l].reshape(1, -1)

        tm_proj = _pick(t, 1024)
        rest, small = _proj_rest(x2, w_rest, w_small, tm_proj, 512)
        xaug = _forget_cumsum(small, bias_f, batch, seq, _pick(seq, 512))
        qkv = _proj_fox(x2, xaug, w_fox, w_aug, tm_proj, 512)
        fox = _fox_attention(qkv, batch, seq, _pick(seq, 512))
        gla = _gla(rest, small, w_g2, b_g2, norm_g, batch, seq, _pick(seq, 256), 64)
        x1 = _merge(x2, fox, gla, rest, w_br_fox[l].astype(BF16), w_br_gla[l].astype(BF16),
                    w_o[l].astype(BF16), ln1_g[l].reshape(1, d), ln1_b[l].reshape(1, d),
                    alpha, _pick(t, 512))

        wq_t = peer_wq[l].T.astype(BF16)
        subkeys = peer_subkeys[l].reshape(2 * PEER_HEADS, PEER_NKEYS, PEER_HALF).astype(BF16)
        rk, bs, lc, ac = _peer_query(x1, wq_t, subkeys, _pick(t, 256))
        x2 = _peer_dense(x1, peer_u[l].astype(BF16), peer_v[l].T.astype(BF16), rk, bs, lc, ac,
                         ln2_g[l].reshape(1, d), ln2_b[l].reshape(1, d), alpha,
                         _pick(t, 512), 512, 256)
    return x2.reshape(batch, seq, d)
```

```python
import functools

import numpy as np
import jax
import jax.numpy as jnp
from jax import lax
from jax.experimental import pallas as pl
from jax.experimental.pallas import tpu as pltpu

F32 = jnp.float32
BF16 = jnp.bfloat16

D_MODEL = 1024
FOX_HEADS = 16
FOX_HEAD_DIM = 64
GLA_HEADS = 4
GLA_HK = 128
GLA_HV = 256
GLA_GATE_RANK = 16
GLA_GATE_NORMALIZER = 16.0
PEER_HEADS = 8
PEER_NKEYS = 128
PEER_HALF = 128
PEER_TOPK = 16
LN_EPS = 1e-5
RMS_EPS = 1e-6

LANES = 128
VMEM_LIMIT = 56 * 1024 * 1024

AUG_HI, AUG_MID, AUG_LO, AUG_ONE = 0, 16, 32, 48
NEG_BIG = -1e30
LOG2E = 1.4426950408889634


def _cparams(*sem):
    return pltpu.CompilerParams(dimension_semantics=sem, vmem_limit_bytes=VMEM_LIMIT)


def _log_sigmoid(z):
    return jnp.minimum(z, 0.0) - jnp.log1p(jnp.exp(-jnp.abs(z)))


def _layer_norm(r, g, b):
    mu = jnp.mean(r, axis=-1, keepdims=True)
    d = r - mu
    var = jnp.mean(d * d, axis=-1, keepdims=True)
    return d * lax.rsqrt(var + LN_EPS) * g + b


def _dot(a, b):
    return jnp.dot(a, b, preferred_element_type=F32)


def _dot_nt(a, b):
    return lax.dot_general(a, b, (((1,), (1,)), ((), ())), preferred_element_type=F32)


def _dot_tn(a, b):
    return lax.dot_general(a, b, (((0,), (0,)), ((), ())), preferred_element_type=F32)


def _proj_kernel(x_ref, w_ref, ws_ref, o_ref, os_ref):
    xb = x_ref[...].astype(BF16)
    o_ref[...] = _dot(xb, w_ref[...]).astype(o_ref.dtype)

    @pl.when(pl.program_id(1) == 0)
    def _():
        os_ref[...] = _dot(xb, ws_ref[...])


def _proj(x2, w_all, w_small, tm, tn):
    t, d = x2.shape
    n = w_all.shape[1]
    return pl.pallas_call(
        _proj_kernel,
        grid=(t // tm, n // tn),
        in_specs=[
            pl.BlockSpec((tm, d), lambda i, j: (i, 0)),
            pl.BlockSpec((d, tn), lambda i, j: (0, j)),
            pl.BlockSpec((d, LANES), lambda i, j: (0, 0)),
        ],
        out_specs=[
            pl.BlockSpec((tm, tn), lambda i, j: (i, j)),
            pl.BlockSpec((tm, LANES), lambda i, j: (i, 0)),
        ],
        out_shape=[
            jax.ShapeDtypeStruct((t, n), BF16),
            jax.ShapeDtypeStruct((t, LANES), F32),
        ],
        compiler_params=_cparams("parallel", "arbitrary"),
        name="proj",
    )(x2, w_all, w_small)


def _cumsum_kernel(s_ref, bias_ref, pq_ref, pk_ref, oq_ref, ok_ref, carry_ref, *, tc):
    @pl.when(pl.program_id(1) == 0)
    def _():
        carry_ref[...] = jnp.zeros_like(carry_ref)

    z = s_ref[...] + bias_ref[...]
    lane = lax.broadcasted_iota(jnp.int32, z.shape, 1)
    logf = jnp.where(lane < FOX_HEADS, _log_sigmoid(z), 0.0)
    row = lax.broadcasted_iota(jnp.int32, (tc, tc), 0)
    col = lax.broadcasted_iota(jnp.int32, (tc, tc), 1)
    ltri = (row >= col).astype(F32)
    c = jnp.dot(ltri, logf, precision=lax.Precision.HIGHEST,
                preferred_element_type=F32) + carry_ref[0:1, :]
    carry_ref[...] = jnp.broadcast_to(c[tc - 1:tc, :], carry_ref.shape)
    c = c * LOG2E
    c_mid = pltpu.roll(c, AUG_MID, 1)
    c_lo = pltpu.roll(c, AUG_LO, 1)
    csel = jnp.where(lane < AUG_MID, c, jnp.where(lane < AUG_LO, c_mid, c_lo))
    hi = csel.astype(BF16).astype(F32)
    r1 = csel - hi
    mid = r1.astype(BF16).astype(F32)
    lo = (r1 - mid).astype(BF16).astype(F32)
    val = jnp.where(lane < AUG_MID, hi,
                    jnp.where(lane < AUG_LO, mid,
                              jnp.where(lane < AUG_ONE, lo,
                                        jnp.where(lane == AUG_ONE, 1.0, 0.0)))).astype(BF16)
    oq_ref[...] = _dot(val, pq_ref[...]).astype(oq_ref.dtype)
    ok_ref[...] = _dot(val, pk_ref[...]).astype(ok_ref.dtype)


def _forget_cumsum(small, bias, place_q, place_k, batch, seq, tc):
    t = small.shape[0]
    nb = seq // tc
    n = place_q.shape[1]
    out = jax.ShapeDtypeStruct((t, n), BF16)
    ospec = pl.BlockSpec((tc, n), lambda b, s: (b * nb + s, 0))
    return pl.pallas_call(
        functools.partial(_cumsum_kernel, tc=tc),
        grid=(batch, nb),
        in_specs=[
            pl.BlockSpec((tc, LANES), lambda b, s: (b * nb + s, 0)),
            pl.BlockSpec((1, LANES), lambda b, s: (0, 0)),
            pl.BlockSpec((LANES, n), lambda b, s: (0, 0)),
            pl.BlockSpec((LANES, n), lambda b, s: (0, 0)),
        ],
        out_specs=[ospec, ospec],
        out_shape=[out, out],
        scratch_shapes=[pltpu.VMEM((8, LANES), F32)],
        compiler_params=_cparams("parallel", "arbitrary"),
        name="forget_cumsum",
    )(small, bias, place_q, place_k)


def _fox_kernel(q_ref, aq_ref, k_ref, ak_ref, v_ref, o_ref, ka_ref, va_ref, m_ref, acc_ref,
                *, tq):
    qi = pl.program_id(2)
    lane = lax.broadcasted_iota(jnp.int32, (tq, LANES), 1)
    low = lane < FOX_HEAD_DIM

    @pl.when(qi == 0)
    def _():
        lane_s = lax.broadcasted_iota(jnp.int32, k_ref.shape, 1)
        low_s = lane_s < FOX_HEAD_DIM
        k = k_ref[...].astype(F32)
        ak = ak_ref[...].astype(F32)
        v = v_ref[...].astype(F32)
        ka_ref[0] = jnp.where(low_s, k, ak).astype(BF16)
        ka_ref[1] = jnp.where(low_s, ak, k).astype(BF16)
        va_ref[0] = jnp.where(low_s, v, (lane_s == FOX_HEAD_DIM).astype(F32)).astype(BF16)
        va_ref[1] = jnp.where(low_s, (lane_s == 0).astype(F32), v).astype(BF16)

    q = q_ref[...].astype(F32)
    aq = aq_ref[...].astype(F32)
    qa = (jnp.where(low, q, aq).astype(BF16), jnp.where(low, aq, q).astype(BF16))
    row = lax.broadcasted_iota(jnp.int32, (tq, tq), 0)
    col = lax.broadcasted_iota(jnp.int32, (tq, tq), 1)
    causal = row >= col
    m_ref[...] = jnp.full(m_ref.shape, NEG_BIG, F32)
    acc_ref[...] = jnp.zeros(acc_ref.shape, F32)

    def step(kb, masked):
        start = pl.multiple_of(kb * tq, tq)
        for hh in range(2):
            k = ka_ref[hh, pl.ds(start, tq), :]
            v = va_ref[hh, pl.ds(start, tq), :]
            s = _dot_nt(qa[hh], k)
            if masked:
                s = jnp.where(causal, s, NEG_BIG)
            m_prev = m_ref[hh]
            m_next = jnp.maximum(m_prev, jnp.max(s, axis=1, keepdims=True))
            p = jnp.exp2(s - jnp.tile(m_next, (1, tq // LANES)))
            alpha = jnp.exp2(m_prev - m_next)
            acc_ref[hh] = acc_ref[hh] * alpha + _dot(p.astype(BF16), v)
            m_ref[hh] = m_next

    def body(kb, carry):
        step(kb, False)
        return carry

    lax.fori_loop(0, qi, body, 0)
    step(qi, True)
    outs = []
    for hh in range(2):
        acc = acc_ref[hh]
        l_lane = FOX_HEAD_DIM if hh == 0 else 0
        l = jnp.sum(jnp.where(lane == l_lane, acc, 0.0), axis=1, keepdims=True)
        outs.append(acc / l)
    o_ref[...] = jnp.where(low, outs[0], outs[1]).astype(o_ref.dtype)


def _fox_attention(proj, augq, augk, batch, seq, tq):
    t = proj.shape[0]
    nq = seq // tq
    npair = FOX_HEADS // 2
    return pl.pallas_call(
        functools.partial(_fox_kernel, tq=tq),
        grid=(batch, npair, nq),
        in_specs=[
            pl.BlockSpec((tq, LANES), lambda b, h, i: (b * nq + i, h)),
            pl.BlockSpec((tq, LANES), lambda b, h, i: (b * nq + i, h)),
            pl.BlockSpec((seq, LANES), lambda b, h, i: (b, npair + h)),
            pl.BlockSpec((seq, LANES), lambda b, h, i: (b, h)),
            pl.BlockSpec((seq, LANES), lambda b, h, i: (b, 2 * npair + h)),
        ],
        out_specs=pl.BlockSpec((tq, LANES), lambda b, h, i: (b * nq + i, h)),
        out_shape=jax.ShapeDtypeStruct((t, FOX_HEADS * FOX_HEAD_DIM), BF16),
        scratch_shapes=[pltpu.VMEM((2, seq, LANES), BF16), pltpu.VMEM((2, seq, LANES), BF16),
                        pltpu.VMEM((2, tq, LANES), F32), pltpu.VMEM((2, tq, LANES), F32)],
        compiler_params=_cparams("parallel", "parallel", "arbitrary"),
        name="fox_attention",
    )(proj, augq, proj, augk, proj)


def _gla_kernel(q_ref, k_ref, v_ref, sm_ref, gog_ref, wg_ref, bg_ref, ng_ref, o_ref,
                st_ref, *, rows, chunk):
    @pl.when(pl.program_id(1) == 0)
    def _():
        st_ref[...] = jnp.zeros_like(st_ref)

    z = _dot(sm_ref[...].astype(BF16), wg_ref[...]) + bg_ref[...]
    log_a = _log_sigmoid(z) * (1.0 / GLA_GATE_NORMALIZER)
    ri = lax.broadcasted_iota(jnp.int32, (chunk, chunk), 0)
    ci = lax.broadcasted_iota(jnp.int32, (chunk, chunk), 1)
    tril = ri >= ci
    ltri = tril.astype(F32)
    for c in range(rows // chunk):
        sl = slice(c * chunk, (c + 1) * chunk)
        bc_all = jnp.dot(ltri, log_a[sl], precision=lax.Precision.HIGHEST,
                         preferred_element_type=F32)
        for h in range(GLA_HEADS):
            kl = slice(h * GLA_HK, (h + 1) * GLA_HK)
            vl = slice(h * GLA_HV, (h + 1) * GLA_HV)
            bc = bc_all[:, kl]
            b_last = bc[chunk - 1:chunk, :]
            kf = k_ref[sl, kl].astype(F32)
            qe = (q_ref[sl, kl].astype(F32) * (GLA_HK ** -0.5) * jnp.exp(bc)).astype(BF16)
            kd = (kf * jnp.exp(-bc)).astype(BF16)
            kdec = (kf * jnp.exp(b_last - bc)).astype(BF16)
            v = v_ref[sl, vl]
            attn = jnp.where(tril, _dot_nt(qe, kd), 0.0)
            st = st_ref[h]
            o = _dot(attn.astype(BF16), v) + _dot_nt(qe, st.astype(BF16))
            st_ref[h] = st * jnp.exp(b_last) + _dot_tn(v, kdec)
            ms = jnp.mean(o * o, axis=-1, keepdims=True)
            gate = gog_ref[sl, vl].astype(F32)
            y = o * lax.rsqrt(ms + RMS_EPS) * ng_ref[...] * (gate * jax.nn.sigmoid(gate))
            o_ref[sl, vl] = y.astype(o_ref.dtype)


def _gla(proj, off, small, w_g2, b_g2, norm_g, batch, seq, rows, chunk):
    t = proj.shape[0]
    nr = seq // rows
    wk = GLA_HEADS * GLA_HK
    wv = GLA_HEADS * GLA_HV
    row = lambda b, s: b * nr + s
    return pl.pallas_call(
        functools.partial(_gla_kernel, rows=rows, chunk=chunk),
        grid=(batch, nr),
        in_specs=[
            pl.BlockSpec((rows, wk), lambda b, s: (row(b, s), off // wk)),
            pl.BlockSpec((rows, wk), lambda b, s: (row(b, s), off // wk + 1)),
            pl.BlockSpec((rows, wv), lambda b, s: (row(b, s), (off + 2 * wk) // wv)),
            pl.BlockSpec((rows, LANES), lambda b, s: (row(b, s), 0)),
            pl.BlockSpec((rows, wv), lambda b, s: (row(b, s), (off + 2 * wk) // wv + 1)),
            pl.BlockSpec((LANES, wk), lambda b, s: (0, 0)),
            pl.BlockSpec((1, wk), lambda b, s: (0, 0)),
            pl.BlockSpec((1, GLA_HV), lambda b, s: (0, 0)),
        ],
        out_specs=pl.BlockSpec((rows, wv), lambda b, s: (row(b, s), 0)),
        out_shape=jax.ShapeDtypeStruct((t, wv), BF16),
        scratch_shapes=[pltpu.VMEM((GLA_HEADS, GLA_HV, GLA_HK), F32)],
        compiler_params=_cparams("parallel", "arbitrary"),
        name="gla",
    )(proj, proj, proj, small, proj, w_g2, b_g2, norm_g)


def _merge_kernel(x_ref, fox_ref, gla_ref, zf_ref, zg_ref, wf_ref, wg_ref, wo_ref,
                  g_ref, b_ref, o_ref, *, alpha):
    a = _dot(fox_ref[...], wf_ref[...])
    b = _dot(gla_ref[...], wg_ref[...])
    merged = (jax.nn.sigmoid(zf_ref[...].astype(F32)) * a
              + jax.nn.sigmoid(zg_ref[...].astype(F32)) * b)
    y = _dot(merged.astype(BF16), wo_ref[...])
    o_ref[...] = _layer_norm(alpha * x_ref[...] + y, g_ref[...], b_ref[...])


def _merge(x2, fox, gla, proj, wf, wg, wo, ln_g, ln_b, alpha, tm):
    t, d = x2.shape
    zf_blk = (proj.shape[1] - 2 * d) // d
    full = lambda i: (0, 0)
    return pl.pallas_call(
        functools.partial(_merge_kernel, alpha=alpha),
        grid=(t // tm,),
        in_specs=[
            pl.BlockSpec((tm, d), lambda i: (i, 0)),
            pl.BlockSpec((tm, d), lambda i: (i, 0)),
            pl.BlockSpec((tm, d), lambda i: (i, 0)),
            pl.BlockSpec((tm, d), lambda i: (i, zf_blk)),
            pl.BlockSpec((tm, d), lambda i: (i, zf_blk + 1)),
            pl.BlockSpec((d, d), full),
            pl.BlockSpec((d, d), full),
            pl.BlockSpec((d, d), full),
            pl.BlockSpec((1, d), full),
            pl.BlockSpec((1, d), full),
        ],
        out_specs=pl.BlockSpec((tm, d), lambda i: (i, 0)),
        out_shape=jax.ShapeDtypeStruct((t, d), F32),
        compiler_params=_cparams("parallel"),
        name="merge_ln",
    )(x2, fox, gla, proj, proj, wf, wg, wo, ln_g, ln_b)


def _top_rows(s, k, with_rank=False):
    tm = s.shape[1]
    row = lax.broadcasted_iota(jnp.int32, (k, tm), 0)
    out = jnp.full((k, tm), -jnp.inf, F32)
    rank = jnp.full(s.shape, float(k), F32)
    cur = s
    for r in range(k):
        m = jnp.max(cur, axis=0, keepdims=True)
        out = jnp.where(row == r, m, out)
        hit = cur == m
        if with_rank:
            rank = jnp.where(hit, float(r), rank)
        cur = jnp.where(hit, -jnp.inf, cur)
    return (out, rank) if with_rank else out


def _peerq_kernel(x_ref, wq_ref, sk_ref, rk_ref, bs_ref, lc_ref, ac_ref, *, tm):
    xb = x_ref[...].astype(BF16)
    qt = _dot_nt(wq_ref[...], xb)
    row8 = lax.broadcasted_iota(jnp.int32, (8, tm), 0)
    for h in range(PEER_HEADS):
        q1 = qt[(2 * h) * PEER_HALF:(2 * h + 1) * PEER_HALF].astype(BF16)
        q2 = qt[(2 * h + 1) * PEER_HALF:(2 * h + 2) * PEER_HALF].astype(BF16)
        s1 = _dot(sk_ref[2 * h], q1)
        s2 = _dot(sk_ref[2 * h + 1], q2)
        t1 = _top_rows(s1, PEER_TOPK)
        t2, rank2 = _top_rows(s2, PEER_TOPK, with_rank=True)
        blocks = [t1[0:1] + t2]
        for a in range(1, 8):
            nb = PEER_TOPK // (a + 1)
            blocks.append(jnp.where(row8 < nb, t1[a:a + 1] + t2[0:8], -jnp.inf))
        blocks.append(t1[8:16] + t2[0:1])
        cand = jnp.concatenate(blocks, axis=0)
        best = _top_rows(cand, PEER_TOPK)
        tau = best[PEER_TOPK - 1:PEER_TOPK]
        zsum = jnp.sum(jnp.exp(best - best[0:1]), axis=0, keepdims=True)
        count = jnp.zeros(s1.shape, F32)
        for a in range(PEER_TOPK):
            t1a = t1[a:a + 1]
            sel = (t1a + t2 >= tau).astype(F32)
            count = jnp.where(s1 == t1a, jnp.sum(sel, axis=0, keepdims=True), count)
        rk_ref[h] = rank2.astype(rk_ref.dtype)
        bs_ref[h] = jnp.exp(s2 - t2[0:1]).astype(bs_ref.dtype)
        lc_ref[h] = count
        ac_ref[h] = jnp.exp(s1 - t1[0:1]) / zsum


def _peer_query(x1, wq_t, subkeys, tm):
    t, d = x1.shape
    nq = wq_t.shape[0]
    shape = (PEER_HEADS, PEER_NKEYS, t)
    ospec = pl.BlockSpec((PEER_HEADS, PEER_NKEYS, tm), lambda i: (0, 0, i))
    return pl.pallas_call(
        functools.partial(_peerq_kernel, tm=tm),
        grid=(t // tm,),
        in_specs=[
            pl.BlockSpec((tm, d), lambda i: (i, 0)),
            pl.BlockSpec((nq, d), lambda i: (0, 0)),
            pl.BlockSpec((2 * PEER_HEADS, PEER_NKEYS, PEER_HALF), lambda i: (0, 0, 0)),
        ],
        out_specs=[ospec, ospec, ospec, ospec],
        out_shape=[jax.ShapeDtypeStruct(shape, BF16), jax.ShapeDtypeStruct(shape, BF16),
                   jax.ShapeDtypeStruct(shape, F32), jax.ShapeDtypeStruct(shape, F32)],
        compiler_params=_cparams("parallel"),
        name="peer_query",
    )(x1, wq_t, subkeys)


def _peer_dense_kernel(x_ref, u_ref, vt_ref, rk_ref, bs_ref, lcp_ref, lcc_ref, acp_ref, acc_ref_,
                       g_ref, b_ref, o_ref, xt_ref, acc_ref, ht0_ref, ht1_ref, act0_ref, act1_ref,
                       *, tb, tn, alpha):
    s = pl.program_id(1)
    nsub = 2 * (pl.num_programs(1) - 1)
    nblk = tb // PEER_NKEYS
    tm = x_ref.shape[0]
    ht_refs = (ht0_ref, ht1_ref)
    act_refs = (act0_ref, act1_ref)

    @pl.when(s == 0)
    def _():
        xt_ref[...] = x_ref[...].T.astype(BF16)
        acc_ref[...] = jnp.zeros_like(acc_ref)
        for ref in ht_refs + act_refs:
            ref[...] = jnp.zeros_like(ref)

    def stage_a(slot):
        ht_refs[slot][...] = _dot(u_ref[slot * tb:(slot + 1) * tb, :], xt_ref[...])

    def rows16(row):
        r16 = jnp.broadcast_to(row, (16, row.shape[1])).astype(BF16)
        return jnp.tile(r16, (PEER_NKEYS // 16, 1))

    def stage_b(slot, k, j, c):
        live = jnp.logical_and(k >= 0, k < nsub).astype(F32)
        lc_ref, ac_ref, i1 = (lcp_ref, acp_ref, nblk + j) if slot == 1 else (lcc_ref, acc_ref_, j)
        rows = slice(j * PEER_NKEYS, (j + 1) * PEER_NKEYS)
        cols = slice(c * tn, (c + 1) * tn)
        hj = ht_refs[slot][rows, cols].astype(BF16)
        g = (0.5 * hj) * (1.0 + lax.erf(hj * np.sqrt(0.5).astype(BF16)))
        w = jnp.zeros(hj.shape, BF16)
        for h in range(PEER_HEADS):
            cnt = rows16(lc_ref[h, i1:i1 + 1, cols])
            a = rows16(ac_ref[h, i1:i1 + 1, cols] * live)
            w = w + jnp.where(rk_ref[h, :, cols] < cnt, bs_ref[h, :, cols],
                              jnp.zeros((), BF16)) * a
        act_refs[slot][rows, cols] = g * w

    def stage_c(slot):
        acc_ref[...] += _dot(vt_ref[:, slot * tb:(slot + 1) * tb], act_refs[slot][...])

    for slot_b, k in ((1, 2 * s - 1), (0, 2 * s)):
        stage_c(1 - slot_b)
        for j in range(nblk):
            for c in range(tm // tn):
                stage_b(slot_b, k, j, c)
        stage_a(1 - slot_b)

    @pl.when(s == pl.num_programs(1) - 1)
    def _():
        r = alpha * x_ref[...] + acc_ref[...].T
        o_ref[...] = _layer_norm(r, g_ref[...], b_ref[...])


def _peer_dense(x1, u, vt, rk, bs, lc, ac, ln_g, ln_b, alpha, tm, tb, tn):
    t, d = x1.shape
    ns = u.shape[0] // (2 * tb)
    sel = pl.BlockSpec((PEER_HEADS, PEER_NKEYS, tm), lambda i, s: (0, 0, i))
    nrow = 2 * tb // PEER_NKEYS
    row_prev = pl.BlockSpec((PEER_HEADS, nrow, tm), lambda i, s: (0, jnp.maximum(s - 1, 0), i))
    row_cur = pl.BlockSpec((PEER_HEADS, nrow, tm), lambda i, s: (0, jnp.minimum(s, ns - 1), i))
    return pl.pallas_call(
        functools.partial(_peer_dense_kernel, tb=tb, tn=tn, alpha=alpha),
        grid=(t // tm, ns + 1),
        in_specs=[
            pl.BlockSpec((tm, d), lambda i, s: (i, 0)),
            pl.BlockSpec((2 * tb, d), lambda i, s: (jnp.minimum(s, ns - 1), 0)),
            pl.BlockSpec((d, 2 * tb), lambda i, s: (0, jnp.maximum(s - 1, 0))),
            sel, sel, row_prev, row_cur, row_prev, row_cur,
            pl.BlockSpec((1, d), lambda i, s: (0, 0)),
            pl.BlockSpec((1, d), lambda i, s: (0, 0)),
        ],
        out_specs=pl.BlockSpec((tm, d), lambda i, s: (i, 0)),
        out_shape=jax.ShapeDtypeStruct((t, d), F32),
        scratch_shapes=[
            pltpu.VMEM((d, tm), BF16),
            pltpu.VMEM((d, tm), F32),
            pltpu.VMEM((tb, tm), F32),
            pltpu.VMEM((tb, tm), F32),
            pltpu.VMEM((tb, tm), BF16),
            pltpu.VMEM((tb, tm), BF16),
        ],
        compiler_params=_cparams("parallel", "arbitrary"),
        name="peer_dense",
    )(x1, u, vt, rk, bs, lc, lc, ac, ac, ln_g, ln_b)


def _aug_placement():
    pq = np.zeros((LANES, FOX_HEADS // 2 * LANES), np.float32)
    pk = np.zeros_like(pq)
    for h in range(FOX_HEADS):
        base = (h // 2) * LANES + (FOX_HEAD_DIM if h % 2 == 0 else 0)
        for j, part in enumerate((AUG_HI, AUG_MID, AUG_LO)):
            pq[part + h, base + j] = 1.0
            pq[AUG_ONE, base + 3 + j] = 1.0
            pk[AUG_ONE, base + j] = 1.0
            pk[part + h, base + 3 + j] = -1.0
    return pq, pk


def _split_w_in(w):
    d = D_MODEL
    fw = FOX_HEADS * FOX_HEAD_DIM
    gk = GLA_HEADS * GLA_HK
    gv = GLA_HEADS * GLA_HV
    sizes = (fw, fw, fw, FOX_HEADS, gk, gk, gv, GLA_GATE_RANK, gv, d, d)
    pts = np.cumsum(sizes)[:-1]
    return jnp.split(w, [int(p) for p in pts], axis=1)


def _tiles(t, seq):
    pick = lambda n, pref: pref if n % pref == 0 else n
    return dict(
        proj_tm=pick(t, 1024), proj_tn=512,
        cumsum_rows=pick(seq, 512),
        fox_tq=pick(seq, 512),
        gla_rows=pick(seq, 256), gla_chunk=64,
        merge_tm=pick(t, 512),
        peerq_tm=pick(t, 256),
        dense_tm=pick(t, 512), dense_tb=512, dense_tn=256,
    )


def kernel(x, w_in, b_fgate, w_gla_g2, b_gla_g2, gla_norm_g, w_br_fox, w_br_gla, w_o,
           ln1_g, ln1_b, peer_wq, peer_subkeys, peer_u, peer_v, ln2_g, ln2_b):
    batch, seq, d = x.shape
    depth = w_in.shape[0]
    t = batch * seq
    alpha = float((2.0 * depth) ** 0.25)
    tl = _tiles(t, seq)
    place_q, place_k = (jnp.asarray(p, BF16) for p in _aug_placement())

    x2 = x.reshape(t, d)
    for l in range(depth):
        fq, fk, fv, ff, gq, gk, gv, gg1, gog, zf, zg = _split_w_in(w_in[l])
        w_all = jnp.concatenate([fq * (FOX_HEAD_DIM ** -0.5 * LOG2E), fk, fv,
                                 gq, gk, gv, gog, zf, zg], axis=1).astype(BF16)
        gla_off = 3 * FOX_HEADS * FOX_HEAD_DIM
        w_small = jnp.pad(jnp.concatenate([ff, gg1], axis=1),
                          ((0, 0), (0, LANES - FOX_HEADS - GLA_GATE_RANK))).astype(BF16)
        bias_f = jnp.pad(b_fgate[l], (0, LANES - FOX_HEADS)).reshape(1, LANES)
        w_g2 = jnp.pad(w_gla_g2[l], ((FOX_HEADS, LANES - FOX_HEADS - GLA_GATE_RANK),
                                     (0, 0))).astype(BF16)
        b_g2 = b_gla_g2[l].reshape(1, -1)
        norm_g = gla_norm_g[l].reshape(1, -1)

        proj, small = _proj(x2, w_all, w_small, tl["proj_tm"], tl["proj_tn"])
        augq, augk = _forget_cumsum(small, bias_f, place_q, place_k, batch, seq,
                                    tl["cumsum_rows"])
        fox = _fox_attention(proj, augq, augk, batch, seq, tl["fox_tq"])
        gla = _gla(proj, gla_off, small, w_g2, b_g2, norm_g, batch, seq,
                   tl["gla_rows"], tl["gla_chunk"])
        x1 = _merge(x2, fox, gla, proj, w_br_fox[l].astype(BF16), w_br_gla[l].astype(BF16),
                    w_o[l].astype(BF16), ln1_g[l].reshape(1, d), ln1_b[l].reshape(1, d),
                    alpha, tl["merge_tm"])

        wq_t = peer_wq[l].T.astype(BF16)
        subkeys = peer_subkeys[l].reshape(2 * PEER_HEADS, PEER_NKEYS, PEER_HALF).astype(BF16)
        rk, bs, lc, ac = _peer_query(x1, wq_t, subkeys, tl["peerq_tm"])
        x2 = _peer_dense(x1, peer_u[l].astype(BF16), peer_v[l].T.astype(BF16), rk, bs, lc, ac,
                         ln2_g[l].reshape(1, d), ln2_b[l].reshape(1, d), alpha,
                         tl["dense_tm"], tl["dense_tb"], tl["dense_tn"])
    return x2.reshape(batch, seq, d)
```
